```python
import math
import jax
import jax.numpy as jnp
from jax import lax
import numpy as np

D_MODEL = 2048
BATCH = 8
SEQ = 2048
DEPTH = 2

N_MIXERS = 2
RMS_EPS = 1e-6

SSM_EXPAND = 2
SSM_D_INNER = SSM_EXPAND * D_MODEL
SSM_HEAD_DIM = 64
SSM_N_HEADS = SSM_D_INNER // SSM_HEAD_DIM
SSM_D_STATE = 128
SSM_N_GROUPS = 8
SSM_CONV_WIDTH = 4
SSM_CHUNK = 128
SSM_CONV_DIM = SSM_D_INNER + 2 * SSM_N_GROUPS * SSM_D_STATE
SSM_IN_DIM = 2 * SSM_D_INNER + 2 * SSM_N_GROUPS * SSM_D_STATE + SSM_N_HEADS

ATTN_HEAD_DIM = 128
ATTN_GROUP_HEADS = 8
ATTN_PATTERNS = ((128, 1), (512, 4), (2048, 16))
ATTN_N_GROUPS = len(ATTN_PATTERNS)
ATTN_BLOCK = 128
ATTN_QKV_DIM = ATTN_N_GROUPS * 3 * ATTN_GROUP_HEADS * ATTN_HEAD_DIM
ATTN_OUT_DIM = ATTN_GROUP_HEADS * ATTN_HEAD_DIM
ROPE_THETA = 10000.0

D_FF = -(-8 * D_MODEL // (3 * 256)) * 256

N_SSM_LAYERS = (DEPTH + 1) // 2
N_ATTN_LAYERS = DEPTH // 2

kernel_name = "hybrid_ssd_dilated_swa_swiglu"


def rms_norm(x, gain):
    xf = x.astype(jnp.float32)
    y = xf * lax.rsqrt(jnp.mean(xf * xf, axis=-1, keepdims=True) + RMS_EPS)
    return (y * gain.astype(jnp.float32)).astype(x.dtype)


def causal_depthwise_conv(u, w, b):
    k_width = w.shape[0]
    out = lax.conv_general_dilated(
        u, w[:, None, :].astype(u.dtype), window_strides=(1,), padding=[(k_width - 1, 0)],
        dimension_numbers=('NWC', 'WIO', 'NWC'), feature_group_count=u.shape[-1])
    return out + b


def ssd_chunked(x, a, b, c):
    out_dtype = x.dtype
    f32 = jnp.float32
    bsz, seq, n_heads, head_dim = x.shape
    n_groups, d_state = b.shape[2], b.shape[3]
    reps = n_heads // n_groups
    n_chunks = seq // SSM_CHUNK
    x = x.astype(f32).reshape(bsz, n_chunks, SSM_CHUNK, n_groups, reps, head_dim)
    a = a.astype(f32).reshape(bsz, n_chunks, SSM_CHUNK, n_groups, reps).transpose(0, 3, 4, 1, 2)
    b = b.astype(f32).reshape(bsz, n_chunks, SSM_CHUNK, n_groups, d_state)
    c = c.astype(f32).reshape(bsz, n_chunks, SSM_CHUNK, n_groups, d_state)
    a_cum = jnp.cumsum(a, axis=-1)
    causal = jnp.tril(jnp.ones((SSM_CHUNK, SSM_CHUNK), dtype=bool))
    seg = a_cum[..., :, None] - a_cum[..., None, :]
    decay = jnp.exp(jnp.where(causal, seg, -jnp.inf))
    cb = jnp.einsum('bclgn,bcsgn->bgcls', c, b)
    y_diag = jnp.einsum('bgrcls,bcsgrp->bclgrp', decay * cb[:, :, None], x)
    decay_to_end = jnp.exp(a_cum[..., -1:] - a_cum)
    chunk_states = jnp.einsum('bclgn,bgrcl,bclgrp->bcgrpn', b, decay_to_end, x)
    chunk_decay = jnp.exp(a_cum[..., -1])

    def step(state, inp):
        new_states, dec = inp
        return state * dec[..., None, None] + new_states, state

    init = jnp.zeros((bsz, n_groups, reps, head_dim, d_state), f32)
    _, prev_states = lax.scan(step, init, (chunk_states.transpose(1, 0, 2, 3, 4, 5),
                                            chunk_decay.transpose(3, 0, 1, 2)))
    prev_states = prev_states.transpose(1, 0, 2, 3, 4, 5)
    y_off = jnp.einsum('bclgn,bcgrpn,bgrcl->bclgrp', c, prev_states, jnp.exp(a_cum))
    return (y_diag + y_off).reshape(bsz, seq, n_heads, head_dim).astype(out_dtype)


def mamba2_mixer(h, w_in, conv_w, conv_b, dt_bias, a_log, d_skip, norm_w, w_out):
    bsz, seq, _ = h.shape
    proj = h @ w_in
    z = proj[..., :SSM_D_INNER]
    xbc = proj[..., SSM_D_INNER:SSM_D_INNER + SSM_CONV_DIM]
    dt_raw = proj[..., SSM_D_INNER + SSM_CONV_DIM:]
    xbc = jax.nn.silu(causal_depthwise_conv(xbc, conv_w, conv_b))
    bc_dim = SSM_N_GROUPS * SSM_D_STATE
    xs = xbc[..., :SSM_D_INNER].reshape(bsz, seq, SSM_N_HEADS, SSM_HEAD_DIM)
    b_in = xbc[..., SSM_D_INNER:SSM_D_INNER + bc_dim].reshape(bsz, seq, SSM_N_GROUPS, SSM_D_STATE)
    c_in = xbc[..., SSM_D_INNER + bc_dim:].reshape(bsz, seq, SSM_N_GROUPS, SSM_D_STATE)
    dt = jax.nn.softplus(dt_raw.astype(jnp.float32) + dt_bias.astype(jnp.float32))
    a = -jnp.exp(a_log.astype(jnp.float32))
    y = ssd_chunked(xs * dt[..., None], dt * a, b_in, c_in)
    y = y + xs * d_skip[:, None]
    y = y.reshape(bsz, seq, SSM_D_INNER) * jax.nn.silu(z)
    group = SSM_D_INNER // SSM_N_GROUPS
    y = rms_norm(y.reshape(bsz, seq, SSM_N_GROUPS, group), norm_w.reshape(SSM_N_GROUPS, group))
    return y.reshape(bsz, seq, SSM_D_INNER) @ w_out


def apply_rotary(t, positions):
    half = ATTN_HEAD_DIM // 2
    inv_freq = ROPE_THETA ** (-jnp.arange(half, dtype=jnp.float32) / half)
    ang = positions[:, None] * inv_freq[None, :]
    cos = jnp.cos(ang)[None, :, None, None, :]
    sin = jnp.sin(ang)[None, :, None, None, :]
    tf = t.astype(jnp.float32)
    t1, t2 = tf[..., :half], tf[..., half:]
    return jnp.concatenate([t1 * cos - t2 * sin, t2 * cos + t1 * sin], axis=-1).astype(t.dtype)


def banded_window_attention(q, k, v, band):
    n, length, heads, hd = q.shape
    n_blocks = -(-length // ATTN_BLOCK)
    pad = n_blocks * ATTN_BLOCK - length

    def blocks(t):
        t = jnp.pad(t, ((0, 0), (0, pad), (0, 0), (0, 0)))
        return t.reshape(n, n_blocks, ATTN_BLOCK, heads, hd)

    def with_prev(t):
        prev = jnp.pad(t, ((0, 0), (1, 0), (0, 0), (0, 0), (0, 0)))[:, :-1]
        return jnp.concatenate([prev, t], axis=2)

    qb = blocks(q).astype(jnp.float32)
    kk = with_prev(blocks(k)).astype(jnp.float32)
    vv = with_prev(blocks(v)).astype(jnp.float32)
    scores = jnp.einsum('nbqhd,nbkhd->nbhqk', qb, kk) * (hd ** -0.5)
    qi = jnp.arange(ATTN_BLOCK)[:, None]
    kj = jnp.arange(2 * ATTN_BLOCK)[None, :]
    dist = ATTN_BLOCK + qi - kj
    blk = jnp.arange(n_blocks)[:, None, None]
    valid = (dist >= 0) & (dist <= band) & ((blk > 0) | (kj >= ATTN_BLOCK))
    scores = jnp.where(valid[None, :, None], scores, -jnp.inf)
    lse = jax.nn.logsumexp(scores, axis=-1)
    probs = jnp.exp(scores - lse[..., None])
    out = jnp.einsum('nbhqk,nbkhd->nbqhd', probs, vv)
    out = out.reshape(n, n_blocks * ATTN_BLOCK, heads, hd)[:, :length]
    lse = lse.transpose(0, 1, 3, 2).reshape(n, n_blocks * ATTN_BLOCK, heads)[:, :length]
    return out, lse


def dilated_window_attention(q, k, v, window, dilation):
    bsz, seq, heads, hd = q.shape
    length = seq // dilation

    def strided(t):
        return t.reshape(bsz, length, dilation, heads, hd).transpose(0, 2, 1, 3, 4).reshape(
            bsz * dilation, length, heads, hd)

    out, lse = banded_window_attention(strided(q), strided(k), strided(v), window // dilation)
    out = out.reshape(bsz, dilation, length, heads, hd).transpose(0, 2, 1, 3, 4).reshape(bsz, seq, heads, hd)
    lse = lse.reshape(bsz, dilation, length, heads).transpose(0, 2, 1, 3).reshape(bsz, seq, heads)
    return out, lse


def dilated_attention_mixer(h, w_qkv, q_norm, k_norm, w_out):
    bsz, seq, _ = h.shape
    qkv = (h @ w_qkv).reshape(bsz, seq, ATTN_N_GROUPS, 3, ATTN_GROUP_HEADS, ATTN_HEAD_DIM)
    positions = jnp.arange(seq, dtype=jnp.float32)
    q = apply_rotary(rms_norm(qkv[:, :, :, 0], q_norm[:, None, :]), positions)
    k = apply_rotary(rms_norm(qkv[:, :, :, 1], k_norm[:, None, :]), positions)
    v = qkv[:, :, :, 2]
    outs, lses = [], []
    for g, (window, dilation) in enumerate(ATTN_PATTERNS):
        o, lse = dilated_window_attention(q[:, :, g], k[:, :, g], v[:, :, g], window, dilation)
        outs.append(o)
        lses.append(lse)
    weights = jax.nn.softmax(jnp.stack(lses), axis=0)
    merged = jnp.einsum('gbsh,gbshd->bshd', weights, jnp.stack(outs)).astype(h.dtype)
    return merged.reshape(bsz, seq, ATTN_OUT_DIM) @ w_out


def swiglu_ffn(h, w_gate, w_up, w_down):
    return (jax.nn.silu(h @ w_gate) * (h @ w_up)) @ w_down


def setup_inputs(seed: int = 0) -> dict:
    key = jax.random.key(seed)
    ks = jax.random.split(key, 20)
    f32 = jnp.float32
    nm, na = N_SSM_LAYERS, N_ATTN_LAYERS

    def normal(k, shape, scale):
        return jax.random.normal(k, shape, f32) * scale

    dt_init = jnp.exp(jax.random.uniform(ks[6], (nm, SSM_N_HEADS), f32,
                                         minval=math.log(1e-3), maxval=math.log(1e-1)))
    return {
        'x': normal(ks[0], (BATCH, SEQ, D_MODEL), 1.0),
        'mix_norm': 1.0 + normal(ks[1], (DEPTH, D_MODEL), 0.02),
        'ffn_norm': 1.0 + normal(ks[2], (DEPTH, D_MODEL), 0.02),
        'ssm_w_in': normal(ks[3], (nm, D_MODEL, SSM_IN_DIM), D_MODEL ** -0.5),
        'ssm_conv_w': normal(ks[4], (nm, SSM_CONV_WIDTH, SSM_CONV_DIM), SSM_CONV_WIDTH ** -0.5),
        'ssm_conv_b': normal(ks[5], (nm, SSM_CONV_DIM), 0.01),
        'ssm_dt_bias': dt_init + jnp.log(-jnp.expm1(-dt_init)),
        'ssm_a_log': jnp.log(jax.random.uniform(ks[7], (nm, SSM_N_HEADS), f32, minval=1.0, maxval=16.0)),
        'ssm_d': 1.0 + normal(ks[8], (nm, SSM_N_HEADS), 0.02),
        'ssm_norm': 1.0 + normal(ks[9], (nm, SSM_D_INNER), 0.02),
        'ssm_w_out': normal(ks[10], (nm, SSM_D_INNER, D_MODEL), SSM_D_INNER ** -0.5),
        'attn_w_qkv': normal(ks[11], (na, D_MODEL, ATTN_QKV_DIM), D_MODEL ** -0.5),
        'attn_q_norm': 1.0 + normal(ks[12], (na, ATTN_N_GROUPS, ATTN_HEAD_DIM), 0.02),
        'attn_k_norm': 1.0 + normal(ks[13], (na, ATTN_N_GROUPS, ATTN_HEAD_DIM), 0.02),
        'attn_w_out': normal(ks[14], (na, ATTN_OUT_DIM, D_MODEL), ATTN_OUT_DIM ** -0.5),
        'ffn_w_gate': normal(ks[15], (DEPTH, D_MODEL, D_FF), D_MODEL ** -0.5),
        'ffn_w_up': normal(ks[16], (DEPTH, D_MODEL, D_FF), D_MODEL ** -0.5),
        'ffn_w_down': normal(ks[17], (DEPTH, D_FF, D_MODEL), D_FF ** -0.5),
    }


def reference(x, mix_norm, ffn_norm, ssm_w_in, ssm_conv_w, ssm_conv_b, ssm_dt_bias, ssm_a_log,
              ssm_d, ssm_norm, ssm_w_out, attn_w_qkv, attn_q_norm, attn_k_norm, attn_w_out,
              ffn_w_gate, ffn_w_up, ffn_w_down):
    for layer in range(DEPTH):
        j = layer // N_MIXERS
        h = rms_norm(x, mix_norm[layer])
        if layer % N_MIXERS == 0:
            mixed = mamba2_mixer(h, ssm_w_in[j], ssm_conv_w[j], ssm_conv_b[j], ssm_dt_bias[j],
                                 ssm_a_log[j], ssm_d[j], ssm_norm[j], ssm_w_out[j])
        else:
            mixed = dilated_attention_mixer(h, attn_w_qkv[j], attn_q_norm[j], attn_k_norm[j], attn_w_out[j])
        x = x + mixed
        h = rms_norm(x, ffn_norm[layer])
        x = x + swiglu_ffn(h, ffn_w_gate[layer], ffn_w_up[layer], ffn_w_down[layer])
    return x
```

```python
import functools
import math

import jax
import jax.numpy as jnp
from jax import lax
from jax.experimental import pallas as pl
from jax.experimental.pallas import tpu as pltpu

F32 = jnp.float32
BF16 = jnp.bfloat16

RMS_EPS = 1e-6
LANES = 128
VMEM_LIMIT_BYTES = 56 * 1024 * 1024

SSM_HEAD_DIM = 64
SSM_D_STATE = 128
SSM_N_GROUPS = 8
SSM_CONV_WIDTH = 4
SSM_CHUNK = 128

ATTN_HEAD_DIM = 128
ATTN_GROUP_HEADS = 8
ATTN_PATTERNS = ((128, 1), (512, 4), (2048, 16))
ATTN_BLOCK = 128
ROPE_THETA = 10000.0


def _params(*sem):
    return pltpu.CompilerParams(dimension_semantics=sem, vmem_limit_bytes=VMEM_LIMIT_BYTES)


def _sigmoid(v):
    return 1.0 / (1.0 + jnp.exp(-v))


def _split3(v):
    hi = v.astype(BF16)
    r1 = v - hi.astype(F32)
    mid = r1.astype(BF16)
    lo = (r1 - mid.astype(F32)).astype(BF16)
    return hi, mid, lo


def _rmsnorm_kernel(x_ref, g_ref, o_ref):
    x = x_ref[...]
    ms = jnp.mean(x * x, axis=-1, keepdims=True)
    o_ref[...] = (x * lax.rsqrt(ms + RMS_EPS) * g_ref[...]).astype(o_ref.dtype)


def rmsnorm(x, gain, tm=512):
    t, d = x.shape
    return pl.pallas_call(
        _rmsnorm_kernel,
        grid=(t // tm,),
        in_specs=[pl.BlockSpec((tm, d), lambda i: (i, 0)),
                  pl.BlockSpec((1, d), lambda i: (0, 0))],
        out_specs=pl.BlockSpec((tm, d), lambda i: (i, 0)),
        out_shape=jax.ShapeDtypeStruct((t, d), BF16),
        compiler_params=_params("parallel"),
        name="rmsnorm",
    )(x, gain.reshape(1, d))


def _rmsnorm_dilated_kernel(x_ref, g_ref, o1_ref, o4_ref, o16_ref, scr):
    x = x_ref[...]
    ms = jnp.mean(x * x, axis=-1, keepdims=True)
    y = x * lax.rsqrt(ms + RMS_EPS) * g_ref[...]
    o1_ref[...] = y.astype(o1_ref.dtype)
    n_slabs = scr.shape[0]
    for s in range(n_slabs):
        scr[s] = y[:, s * LANES:(s + 1) * LANES]
    for o_ref in (o4_ref, o16_ref):
        dil, rows = o_ref.shape[0], o_ref.shape[1]
        for r in range(dil):
            for s in range(n_slabs):
                o_ref[r, :, s * LANES:(s + 1) * LANES] = (
                    scr[s, pl.ds(r, rows, stride=dil), :].astype(o_ref.dtype))


def rmsnorm_dilated(x, gain, bsz, seq, tm=256):
    t, d = x.shape
    tiles = seq // tm
    d4, d16 = ATTN_PATTERNS[1][1], ATTN_PATTERNS[2][1]
    o1, o4, o16 = pl.pallas_call(
        _rmsnorm_dilated_kernel,
        grid=(bsz, tiles),
        in_specs=[pl.BlockSpec((tm, d), lambda b, i: (b * tiles + i, 0)),
                  pl.BlockSpec((1, d), lambda b, i: (0, 0))],
        out_specs=[pl.BlockSpec((tm, d), lambda b, i: (b * tiles + i, 0)),
                   pl.BlockSpec((None, d4, tm // d4, d), lambda b, i: (b, 0, i, 0)),
                   pl.BlockSpec((None, d16, tm // d16, d), lambda b, i: (b, 0, i, 0))],
        out_shape=[jax.ShapeDtypeStruct((t, d), BF16),
                   jax.ShapeDtypeStruct((bsz, d4, seq // d4, d), BF16),
                   jax.ShapeDtypeStruct((bsz, d16, seq // d16, d), BF16)],
        scratch_shapes=[pltpu.VMEM((d // LANES, tm, LANES), F32)],
        compiler_params=_params("parallel", "parallel"),
        name="rmsnorm_dilated",
    )(x, gain.reshape(1, d))
    return o1, o4.reshape(t, d), o16.reshape(t, d)


def _mm_kernel(a_ref, w_ref, o_ref):
    o_ref[...] = jnp.dot(a_ref[...], w_ref[...], preferred_element_type=F32).astype(o_ref.dtype)


def matmul(a, w, out_dtype, tm=1024, tn=512):
    m, k = a.shape
    n = w.shape[1]
    return pl.pallas_call(
        _mm_kernel,
        grid=(m // tm, n // tn),
        in_specs=[pl.BlockSpec((tm, k), lambda i, j: (i, 0)),
                  pl.BlockSpec((k, tn), lambda i, j: (0, j))],
        out_specs=pl.BlockSpec((tm, tn), lambda i, j: (i, j)),
        out_shape=jax.ShapeDtypeStruct((m, n), out_dtype),
        compiler_params=_params("parallel", "parallel"),
        name="matmul",
    )(a, w)


def _mm_res_kernel(a_ref, w_ref, r_ref, o_ref):
    o_ref[...] = r_ref[...] + jnp.dot(a_ref[...], w_ref[...], preferred_element_type=F32)


def matmul_residual(a, w, res, tm=1024, tn=512):
    m, k = a.shape
    n = w.shape[1]
    return pl.pallas_call(
        _mm_res_kernel,
        grid=(m // tm, n // tn),
        in_specs=[pl.BlockSpec((tm, k), lambda i, j: (i, 0)),
                  pl.BlockSpec((k, tn), lambda i, j: (0, j)),
                  pl.BlockSpec((tm, tn), lambda i, j: (i, j))],
        out_specs=pl.BlockSpec((tm, tn), lambda i, j: (i, j)),
        out_shape=jax.ShapeDtypeStruct((m, n), F32),
        compiler_params=_params("parallel", "parallel"),
        name="matmul_residual",
    )(a, w, res)


def _dt_kernel(a_ref, w_ref, b_ref, o_ref):
    raw = jnp.dot(a_ref[...], w_ref[...], preferred_element_type=F32) + b_ref[...]
    o_ref[...] = jnp.maximum(raw, 0.0) + jnp.log(1.0 + jnp.exp(-jnp.abs(raw)))


def dt_projection(a, w, bias, tm=1024):
    m, k = a.shape
    n = w.shape[1]
    return pl.pallas_call(
        _dt_kernel,
        grid=(m // tm,),
        in_specs=[pl.BlockSpec((tm, k), lambda i: (i, 0)),
                  pl.BlockSpec((k, n), lambda i: (0, 0)),
                  pl.BlockSpec((1, n), lambda i: (0, 0))],
        out_specs=pl.BlockSpec((tm, n), lambda i: (i, 0)),
        out_shape=jax.ShapeDtypeStruct((m, n), F32),
        compiler_params=_params("parallel"),
        name="dt_projection",
    )(a, w, bias)


def _conv_proj_kernel(a_ref, w_ref, cw_ref, cb_ref, o_ref, scr, *, tiles_per_seq):
    i = pl.program_id(1)
    tm = a_ref.shape[0]
    halo = 8

    @pl.when(i % tiles_per_seq == 0)
    def _():
        scr[0:halo, :] = jnp.zeros((halo, scr.shape[1]), F32)

    scr[halo:halo + tm, :] = jnp.dot(a_ref[...], w_ref[...], preferred_element_type=F32)
    cw = cw_ref[...]
    y = cb_ref[...]
    for k in range(SSM_CONV_WIDTH):
        shift = SSM_CONV_WIDTH - 1 - k
        y = y + cw[k:k + 1, :] * scr[halo - shift:halo - shift + tm, :]
    o_ref[...] = (y * _sigmoid(y)).astype(o_ref.dtype)
    scr[0:halo, :] = scr[tm:tm + halo, :]


def conv_projection(a, w, conv_w, conv_b, seq, tm=1024, tn=512):
    m, k = a.shape
    n = w.shape[1]
    return pl.pallas_call(
        functools.partial(_conv_proj_kernel, tiles_per_seq=seq // tm),
        grid=(n // tn, m // tm),
        in_specs=[pl.BlockSpec((tm, k), lambda j, i: (i, 0)),
                  pl.BlockSpec((k, tn), lambda j, i: (0, j)),
                  pl.BlockSpec((SSM_CONV_WIDTH, tn), lambda j, i: (0, j)),
                  pl.BlockSpec((1, tn), lambda j, i: (0, j))],
        out_specs=pl.BlockSpec((tm, tn), lambda j, i: (i, j)),
        out_shape=jax.ShapeDtypeStruct((m, n), BF16),
        scratch_shapes=[pltpu.VMEM((tm + 8, tn), F32)],
        compiler_params=_params("parallel", "arbitrary"),
        name="conv_projection",
    )(a, w, conv_w, conv_b.reshape(1, n))


def _ssd_kernel(xbc_ref, z_ref, dt_ref, a_ref, dskip_ref, nw_ref, expand_ref, o_ref, state_ref,
                *, d_inner):
    c = pl.program_id(1)
    n_groups = SSM_N_GROUPS
    gw = d_inner // n_groups
    pair = 2 * SSM_HEAD_DIM
    lq = SSM_CHUNK

    @pl.when(c == 0)
    def _():
        state_ref[...] = jnp.zeros(state_ref.shape, F32)

    dt = dt_ref[...]
    a = dt * a_ref[...]
    row_i = lax.broadcasted_iota(jnp.int32, (lq, lq), 0)
    col_i = lax.broadcasted_iota(jnp.int32, (lq, lq), 1)
    causal = row_i >= col_i
    tril = jnp.where(causal, 1.0, 0.0).astype(BF16)
    a_hi, a_mid, a_lo = _split3(a)
    acum = (jnp.dot(tril, a_hi, preferred_element_type=F32)
            + jnp.dot(tril, a_mid, preferred_element_type=F32)
            + jnp.dot(tril, a_lo, preferred_element_type=F32))
    acum_t = acum.T
    dt_t = dt.T

    expand = expand_ref[...]

    def widen(v):
        hi, mid, lo = _split3(v)
        return (jnp.dot(hi, expand, preferred_element_type=F32)
                + jnp.dot(mid, expand, preferred_element_type=F32)
                + jnp.dot(lo, expand, preferred_element_type=F32))

    acum_w = widen(acum)
    dt_w = widen(dt)
    alast_w = acum_w[lq - 1:lq, :]
    decay_in_w = jnp.exp(acum_w)
    decay_out_w = jnp.exp(alast_w - acum_w)
    chunk_decay_w = jnp.exp(alast_w)
    lane = lax.broadcasted_iota(jnp.int32, (lq, pair), 1)
    lo_half = lane < SSM_HEAD_DIM

    for g in range(n_groups):
        sl = slice(g * gw, (g + 1) * gw)
        xs = xbc_ref[:, sl]
        bm = xbc_ref[:, d_inner + g * SSM_D_STATE:d_inner + (g + 1) * SSM_D_STATE]
        cm = xbc_ref[:, d_inner + (n_groups + g) * SSM_D_STATE:
                     d_inner + (n_groups + g + 1) * SSM_D_STATE]
        xs32 = xs.astype(F32)
        cb = lax.dot_general(cm, bm, (((1,), (1,)), ((), ())), preferred_element_type=F32)
        state = state_ref[g]
        y = jnp.dot(cm, state.astype(BF16), preferred_element_type=F32) * decay_in_w[:, sl]
        xw = (xs32 * (dt_w[:, sl] * decay_out_w[:, sl])).astype(BF16)
        bm_t = bm.astype(F32).T.astype(BF16)
        state_ref[g] = state * chunk_decay_w[:, sl] + jnp.dot(bm_t, xw, preferred_element_type=F32)

        y_pairs = []
        for hp in range(gw // pair):
            xs_pair = xs[:, hp * pair:(hp + 1) * pair]
            zero = jnp.zeros_like(xs_pair)
            acc = None
            for half in range(2):
                h = (g * gw + hp * pair) // SSM_HEAD_DIM + half
                seg = acum[:, h:h + 1] - acum_t[h:h + 1, :]
                decay = jnp.exp(jnp.where(causal, seg, -jnp.inf))
                mat = (decay * cb * dt_t[h:h + 1, :]).astype(BF16)
                xs_half = jnp.where(lo_half if half == 0 else jnp.logical_not(lo_half), xs_pair, zero)
                part = jnp.dot(mat, xs_half, preferred_element_type=F32)
                acc = part if acc is None else acc + part
            y_pairs.append(acc)
        y = y + jnp.concatenate(y_pairs, axis=1)
        y = y + xs32 * dskip_ref[:, sl]
        zg = z_ref[:, sl].astype(F32)
        y = y * (zg * _sigmoid(zg))
        ms = jnp.mean(y * y, axis=-1, keepdims=True)
        o_ref[:, sl] = (y * lax.rsqrt(ms + RMS_EPS) * nw_ref[:, sl]).astype(o_ref.dtype)


def ssd_mixer_core(xbc, z, dt, a_row, dskip_w, norm_w, expand, bsz, seq):
    t, d_inner = z.shape
    n_chunks = seq // SSM_CHUNK
    gw = d_inner // SSM_N_GROUPS
    row = lambda b, c: (b * n_chunks + c, 0)
    fixed = lambda b, c: (0, 0)
    return pl.pallas_call(
        functools.partial(_ssd_kernel, d_inner=d_inner),
        grid=(bsz, n_chunks),
        in_specs=[pl.BlockSpec((SSM_CHUNK, xbc.shape[1]), row),
                  pl.BlockSpec((SSM_CHUNK, d_inner), row),
                  pl.BlockSpec((SSM_CHUNK, LANES), row),
                  pl.BlockSpec((1, LANES), fixed),
                  pl.BlockSpec((1, d_inner), fixed),
                  pl.BlockSpec((1, d_inner), fixed),
                  pl.BlockSpec((LANES, d_inner), fixed)],
        out_specs=pl.BlockSpec((SSM_CHUNK, d_inner), row),
        out_shape=jax.ShapeDtypeStruct((t, d_inner), BF16),
        scratch_shapes=[pltpu.VMEM((SSM_N_GROUPS, SSM_D_STATE, gw), F32)],
        compiler_params=_params("parallel", "arbitrary"),
        name="ssd_mixer_core",
    )(xbc, z, dt, a_row, dskip_w, norm_w, expand)


def _ffn_kernel(h_ref, x_ref, wg_ref, wu_ref, wd_ref, o_ref):
    @pl.when(pl.program_id(1) == 0)
    def _():
        o_ref[...] = x_ref[...]

    h = h_ref[...]
    gate = jnp.dot(h, wg_ref[...], preferred_element_type=F32)
    up = jnp.dot(h, wu_ref[...], preferred_element_type=F32)
    act = (gate * _sigmoid(gate) * up).astype(BF16)
    o_ref[...] += jnp.dot(act, wd_ref[...], preferred_element_type=F32)


def swiglu_ffn_residual(h, x, w_gate, w_up, w_down, tm=512, tf=512):
    m, d = h.shape
    d_ff = w_gate.shape[1]
    return pl.pallas_call(
        _ffn_kernel,
        grid=(m // tm, d_ff // tf),
        in_specs=[pl.BlockSpec((tm, d), lambda i, f: (i, 0)),
                  pl.BlockSpec((tm, d), lambda i, f: (i, 0)),
                  pl.BlockSpec((d, tf), lambda i, f: (0, f)),
                  pl.BlockSpec((d, tf), lambda i, f: (0, f)),
                  pl.BlockSpec((tf, d), lambda i, f: (f, 0))],
        out_specs=pl.BlockSpec((tm, d), lambda i, f: (i, 0)),
        out_shape=jax.ShapeDtypeStruct((m, d), F32),
        compiler_params=_params("parallel", "arbitrary"),
        name="swiglu_ffn",
    )(h, x, w_gate, w_up, w_down)


def _qkv_kernel(a_ref, w_ref, gain_ref, cos_ref, sin_ref, o_ref):
    which = pl.program_id(1)
    acc = jnp.dot(a_ref[...], w_ref[...], preferred_element_type=F32)

    @pl.when(which == 2)
    def _():
        o_ref[...] = acc.astype(o_ref.dtype)

    @pl.when(which < 2)
    def _():
        gain = gain_ref[...]
        cos = cos_ref[...]
        sin = sin_ref[...]
        hd = ATTN_HEAD_DIM
        for h in range(acc.shape[1] // hd):
            t = acc[:, h * hd:(h + 1) * hd]
            ms = jnp.mean(t * t, axis=-1, keepdims=True)
            tn = t * lax.rsqrt(ms + RMS_EPS) * gain
            rot = tn * cos + pltpu.roll(tn, hd // 2, axis=1) * sin
            o_ref[:, h * hd:(h + 1) * hd] = rot.astype(o_ref.dtype)


def qkv_projection(a, w3, gains, cos, sin, seq, tm=1024):
    m, k = a.shape
    n = w3.shape[2]
    tiles = seq // tm
    return pl.pallas_call(
        _qkv_kernel,
        grid=(m // tm, 3),
        in_specs=[pl.BlockSpec((tm, k), lambda i, j: (i, 0)),
                  pl.BlockSpec((None, k, n), lambda i, j: (j, 0, 0)),
                  pl.BlockSpec((None, 1, ATTN_HEAD_DIM), lambda i, j: (jnp.minimum(j, 1), 0, 0)),
                  pl.BlockSpec((tm, ATTN_HEAD_DIM), lambda i, j: (i % tiles, 0)),
                  pl.BlockSpec((tm, ATTN_HEAD_DIM), lambda i, j: (i % tiles, 0))],
        out_specs=pl.BlockSpec((None, tm, n), lambda i, j: (j, i, 0)),
        out_shape=jax.ShapeDtypeStruct((3, m, n), BF16),
        compiler_params=_params("parallel", "arbitrary"),
        name="qkv_projection",
    )(a, w3, gains, cos, sin)


def _attn_kernel(*refs, seq):
    n_groups = len(ATTN_PATTERNS)
    qkv = refs[:3 * n_groups]
    o_ref = refs[3 * n_groups]
    out_scr, lse_scr = refs[3 * n_groups + 1:]
    blk = ATTN_BLOCK
    scale = ATTN_HEAD_DIM ** -0.5
    qi = lax.broadcasted_iota(jnp.int32, (blk, blk), 0)
    kj = lax.broadcasted_iota(jnp.int32, (blk, blk), 1)
    own_ok = kj <= qi
    prev_ok = kj >= qi
    contract_last = (((1,), (1,)), ((), ()))

    for g, (window, dil) in enumerate(ATTN_PATTERNS):
        assert window // dil == blk
        q_ref, k_ref, v_ref = qkv[3 * g:3 * g + 3]
        length = seq // dil
        nb = length // blk
        for r in range(dil):
            for i in range(nb):
                p0 = (r * nb + i) * blk
                q = q_ref[p0:p0 + blk, :]
                s_own = lax.dot_general(q, k_ref[p0:p0 + blk, :], contract_last,
                                        preferred_element_type=F32) * scale
                s_own = jnp.where(own_ok, s_own, -jnp.inf)
                m = jnp.max(s_own, axis=-1, keepdims=True)
                if i > 0:
                    s_prev = lax.dot_general(q, k_ref[p0 - blk:p0, :], contract_last,
                                             preferred_element_type=F32) * scale
                    s_prev = jnp.where(prev_ok, s_prev, -jnp.inf)
                    m = jnp.maximum(m, jnp.max(s_prev, axis=-1, keepdims=True))
                p_own = jnp.exp(s_own - m)
                l = jnp.sum(p_own, axis=-1, keepdims=True)
                acc = jnp.dot(p_own.astype(BF16), v_ref[p0:p0 + blk, :], preferred_element_type=F32)
                if i > 0:
                    p_prev = jnp.exp(s_prev - m)
                    l = l + jnp.sum(p_prev, axis=-1, keepdims=True)
                    acc = acc + jnp.dot(p_prev.astype(BF16), v_ref[p0 - blk:p0, :],
                                        preferred_element_type=F32)
                out = acc / l
                lse = jnp.broadcast_to(m + jnp.log(l), (blk, ATTN_HEAD_DIM))
                if dil == 1:
                    out_scr[g, p0:p0 + blk, :] = out
                    lse_scr[g, p0:p0 + blk, :] = lse
                else:
                    dst = pl.ds(i * blk * dil + r, blk, stride=dil)
                    out_scr[g, dst, :] = out
                    lse_scr[g, dst, :] = lse

    lse_all = [lse_scr[g] for g in range(n_groups)]
    top = functools.reduce(jnp.maximum, lse_all)
    num = jnp.zeros((seq, ATTN_HEAD_DIM), F32)
    den = jnp.zeros((seq, ATTN_HEAD_DIM), F32)
    for g in range(n_groups):
        wgt = jnp.exp(lse_all[g] - top)
        num = num + wgt * out_scr[g]
        den = den + wgt
    o_ref[...] = (num / den).astype(o_ref.dtype)


def dilated_attention(qkv_groups, bsz, seq):
    t = qkv_groups[0].shape[1]
    n = qkv_groups[0].shape[2]
    heads = n // ATTN_HEAD_DIM
    in_specs, args = [], []
    for arr in qkv_groups:
        for which in range(3):
            in_specs.append(pl.BlockSpec((None, seq, ATTN_HEAD_DIM),
                                         lambda b, h, which=which: (which, b, h)))
            args.append(arr)
    n_groups = len(qkv_groups)
    return pl.pallas_call(
        functools.partial(_attn_kernel, seq=seq),
        grid=(bsz, heads),
        in_specs=in_specs,
        out_specs=pl.BlockSpec((seq, ATTN_HEAD_DIM), lambda b, h: (b, h)),
        out_shape=jax.ShapeDtypeStruct((t, n), BF16),
        scratch_shapes=[pltpu.VMEM((n_groups, seq, ATTN_HEAD_DIM), F32),
                        pltpu.VMEM((n_groups, seq, ATTN_HEAD_DIM), F32)],
        compiler_params=_params("parallel", "parallel"),
        name="dilated_attention",
    )(*args)


def _dilate_rows(table, dil):
    seq, c = table.shape
    return table.reshape(seq // dil, dil, c).transpose(1, 0, 2).reshape(seq, c)


def _rope_tables(seq):
    half = ATTN_HEAD_DIM // 2
    inv_freq = ROPE_THETA ** (-jnp.arange(half, dtype=F32) / half)
    ang = jnp.arange(seq, dtype=F32)[:, None] * inv_freq[None, :]
    cos, sin = jnp.cos(ang), jnp.sin(ang)
    return jnp.concatenate([cos, cos], axis=1), jnp.concatenate([-sin, sin], axis=1)


def mamba2_layer(x, h, bsz, seq, w_in, conv_w, conv_b, dt_bias, a_log, d_skip, norm_w, w_out):
    d_inner = w_out.shape[0]
    n_heads = d_inner // SSM_HEAD_DIM
    conv_dim = conv_w.shape[1]
    d_model = w_in.shape[0]
    w_z = w_in[:, :d_inner].astype(BF16)
    w_xbc = w_in[:, d_inner:d_inner + conv_dim].astype(BF16)
    w_dt = jnp.zeros((d_model, LANES), BF16).at[:, :n_heads].set(
        w_in[:, d_inner + conv_dim:].astype(BF16))
    dt_bias_row = jnp.zeros((1, LANES), F32).at[0, :n_heads].set(dt_bias.astype(F32))
    a_row = jnp.zeros((1, LANES), F32).at[0, :n_heads].set(-jnp.exp(a_log.astype(F32)))
    dskip_w = jnp.repeat(d_skip.astype(F32), SSM_HEAD_DIM).reshape(1, d_inner)
    head_of_channel = jnp.arange(d_inner) // SSM_HEAD_DIM
    expand = (jnp.arange(LANES)[:, None] == head_of_channel[None, :]).astype(BF16)

    z = matmul(h, w_z, BF16)
    xbc = conv_projection(h, w_xbc, conv_w.astype(F32), conv_b.astype(F32), seq)
    dt = dt_projection(h, w_dt, dt_bias_row)
    y = ssd_mixer_core(xbc, z, dt, a_row, dskip_w, norm_w.astype(F32).reshape(1, d_inner), expand,
                       bsz, seq)
    return matmul_residual(y, w_out.astype(BF16), x)


def attention_layer(x, hs, bsz, seq, w_qkv, q_norm, k_norm, w_out):
    d_model = w_qkv.shape[0]
    n_groups = len(ATTN_PATTERNS)
    n = ATTN_GROUP_HEADS * ATTN_HEAD_DIM
    w = w_qkv.astype(BF16).reshape(d_model, n_groups, 3, n).transpose(1, 2, 0, 3)
    cos, sin = _rope_tables(seq)
    groups = []
    for g, (_, dil) in enumerate(ATTN_PATTERNS):
        gains = jnp.stack([q_norm[g], k_norm[g]]).astype(F32).reshape(2, 1, ATTN_HEAD_DIM)
        groups.append(qkv_projection(hs[g], w[g], gains, _dilate_rows(cos, dil), _dilate_rows(sin, dil), seq))
    merged = dilated_attention(groups, bsz, seq)
    return matmul_residual(merged, w_out.astype(BF16), x)


def kernel(x, mix_norm, ffn_norm, ssm_w_in, ssm_conv_w, ssm_conv_b, ssm_dt_bias, ssm_a_log, ssm_d, ssm_norm, ssm_w_out, attn_w_qkv, attn_q_norm, attn_k_norm, attn_w_out, ffn_w_gate, ffn_w_up, ffn_w_down):
    bsz, seq, d_model = x.shape
    depth = mix_norm.shape[0]
    xt = x.reshape(bsz * seq, d_model).astype(F32)
    for layer in range(depth):
        j = layer // 2
        if layer % 2 == 0:
            h = rmsnorm(xt, mix_norm[layer].astype(F32))
            xt = mamba2_layer(xt, h, bsz, seq, ssm_w_in[j], ssm_conv_w[j], ssm_conv_b[j], ssm_dt_bias[j],
                              ssm_a_log[j], ssm_d[j], ssm_norm[j], ssm_w_out[j])
        else:
            hs = rmsnorm_dilated(xt, mix_norm[layer].astype(F32), bsz, seq)
            xt = attention_layer(xt, hs, bsz, seq, attn_w_qkv[j], attn_q_norm[j], attn_k_norm[j],
                                 attn_w_out[j])
        h = rmsnorm(xt, ffn_norm[layer].astype(F32))
        xt = swiglu_ffn_residual(h, xt, ffn_w_gate[layer].astype(BF16), ffn_w_up[layer].astype(BF16),
                                 ffn_w_down[layer].astype(BF16))
    return xt.reshape(bsz, seq, d_model).astype(x.dtype)
```

```python
import functools
import math

import jax
import jax.numpy as jnp
from jax import lax
from jax.experimental import pallas as pl
from jax.experimental.pallas import tpu as pltpu

F32 = jnp.float32
BF16 = jnp.bfloat16

RMS_EPS = 1e-6
LANES = 128
VMEM_LIMIT_BYTES = 56 * 1024 * 1024

SSM_HEAD_DIM = 64
SSM_D_STATE = 128
SSM_N_GROUPS = 8
SSM_CONV_WIDTH = 4
SSM_CHUNK = 128

ATTN_HEAD_DIM = 128
ATTN_GROUP_HEADS = 8
ATTN_PATTERNS = ((128, 1), (512, 4), (2048, 16))
ATTN_BLOCK = 128
ROPE_THETA = 10000.0


def _params(*sem):
    return pltpu.CompilerParams(dimension_semantics=sem, vmem_limit_bytes=VMEM_LIMIT_BYTES)


def _sigmoid(v):
    return 1.0 / (1.0 + jnp.exp(-v))


def _split3(v):
    hi = v.astype(BF16)
    r1 = v - hi.astype(F32)
    mid = r1.astype(BF16)
    lo = (r1 - mid.astype(F32)).astype(BF16)
    return hi, mid, lo


def _rmsnorm_kernel(x_ref, g_ref, o_ref):
    x = x_ref[...]
    ms = jnp.mean(x * x, axis=-1, keepdims=True)
    o_ref[...] = (x * lax.rsqrt(ms + RMS_EPS) * g_ref[...]).astype(o_ref.dtype)


def rmsnorm(x, gain, tm=512):
    t, d = x.shape
    return pl.pallas_call(
        _rmsnorm_kernel,
        grid=(t // tm,),
        in_specs=[pl.BlockSpec((tm, d), lambda i: (i, 0)),
                  pl.BlockSpec((1, d), lambda i: (0, 0))],
        out_specs=pl.BlockSpec((tm, d), lambda i: (i, 0)),
        out_shape=jax.ShapeDtypeStruct((t, d), BF16),
        compiler_params=_params("parallel"),
        name="rmsnorm",
    )(x, gain.reshape(1, d))


def _rmsnorm_dilated_kernel(x_ref, g_ref, o1_ref, o4_ref, o16_ref, scr):
    x = x_ref[...]
    ms = jnp.mean(x * x, axis=-1, keepdims=True)
    y = x * lax.rsqrt(ms + RMS_EPS) * g_ref[...]
    o1_ref[...] = y.astype(o1_ref.dtype)
    n_slabs = scr.shape[0]
    for s in range(n_slabs):
        scr[s] = y[:, s * LANES:(s + 1) * LANES]
    for o_ref in (o4_ref, o16_ref):
        dil, rows = o_ref.shape[0], o_ref.shape[1]
        for r in range(dil):
            for s in range(n_slabs):
                o_ref[r, :, s * LANES:(s + 1) * LANES] = (
                    scr[s, pl.ds(r, rows, stride=dil), :].astype(o_ref.dtype))


def rmsnorm_dilated(x, gain, bsz, seq, tm=256):
    t, d = x.shape
    tiles = seq // tm
    d4, d16 = ATTN_PATTERNS[1][1], ATTN_PATTERNS[2][1]
    o1, o4, o16 = pl.pallas_call(
        _rmsnorm_dilated_kernel,
        grid=(bsz, tiles),
        in_specs=[pl.BlockSpec((tm, d), lambda b, i: (b * tiles + i, 0)),
                  pl.BlockSpec((1, d), lambda b, i: (0, 0))],
        out_specs=[pl.BlockSpec((tm, d), lambda b, i: (b * tiles + i, 0)),
                   pl.BlockSpec((None, d4, tm // d4, d), lambda b, i: (b, 0, i, 0)),
                   pl.BlockSpec((None, d16, tm // d16, d), lambda b, i: (b, 0, i, 0))],
        out_shape=[jax.ShapeDtypeStruct((t, d), BF16),
                   jax.ShapeDtypeStruct((bsz, d4, seq // d4, d), BF16),
                   jax.ShapeDtypeStruct((bsz, d16, seq // d16, d), BF16)],
        scratch_shapes=[pltpu.VMEM((d // LANES, tm, LANES), F32)],
        compiler_params=_params("parallel", "parallel"),
        name="rmsnorm_dilated",
    )(x, gain.reshape(1, d))
    return o1, o4.reshape(t, d), o16.reshape(t, d)


def _mm_kernel(a_ref, w_ref, o_ref):
    o_ref[...] = jnp.dot(a_ref[...], w_ref[...], preferred_element_type=F32).astype(o_ref.dtype)


def matmul(a, w, out_dtype, col0=0, n=None, tm=1024, tn=512):
    m, k = a.shape
    n = w.shape[1] if n is None else n
    j0 = col0 // tn
    assert j0 * tn == col0
    return pl.pallas_call(
        _mm_kernel,
        grid=(m // tm, n // tn),
        in_specs=[pl.BlockSpec((tm, k), lambda i, j: (i, 0)),
                  pl.BlockSpec((k, tn), lambda i, j: (0, j0 + j))],
        out_specs=pl.BlockSpec((tm, tn), lambda i, j: (i, j)),
        out_shape=jax.ShapeDtypeStruct((m, n), out_dtype),
        compiler_params=_params("parallel", "parallel"),
        name="matmul",
    )(a, w)


def _mm_res_kernel(a_ref, w_ref, r_ref, o_ref):
    o_ref[...] = r_ref[...] + jnp.dot(a_ref[...], w_ref[...], preferred_element_type=F32)


def matmul_residual(a, w, res, tm=1024, tn=512):
    m, k = a.shape
    n = w.shape[1]
    return pl.pallas_call(
        _mm_res_kernel,
        grid=(m // tm, n // tn),
        in_specs=[pl.BlockSpec((tm, k), lambda i, j: (i, 0)),
                  pl.BlockSpec((k, tn), lambda i, j: (0, j)),
                  pl.BlockSpec((tm, tn), lambda i, j: (i, j))],
        out_specs=pl.BlockSpec((tm, tn), lambda i, j: (i, j)),
        out_shape=jax.ShapeDtypeStruct((m, n), F32),
        compiler_params=_params("parallel", "parallel"),
        name="matmul_residual",
    )(a, w, res)


def _dt_kernel(a_ref, w_ref, b_ref, o_ref):
    raw = jnp.dot(a_ref[...], w_ref[...], preferred_element_type=F32) + b_ref[...]
    o_ref[...] = jnp.maximum(raw, 0.0) + jnp.log(1.0 + jnp.exp(-jnp.abs(raw)))


def dt_projection(a, w, bias, tm=1024):
    m, k = a.shape
    n = w.shape[1]
    return pl.pallas_call(
        _dt_kernel,
        grid=(m // tm,),
        in_specs=[pl.BlockSpec((tm, k), lambda i: (i, 0)),
                  pl.BlockSpec((k, n), lambda i: (0, 0)),
                  pl.BlockSpec((1, n), lambda i: (0, 0))],
        out_specs=pl.BlockSpec((tm, n), lambda i: (i, 0)),
        out_shape=jax.ShapeDtypeStruct((m, n), F32),
        compiler_params=_params("parallel"),
        name="dt_projection",
    )(a, w, bias)


def _conv_proj_kernel(a_ref, w_ref, cw_ref, cb_ref, o_ref, acc0, acc1, tail_scr, *, tiles_per_seq, rows=64):
    i = pl.program_id(1)
    tm = a_ref.shape[0]
    halo = tail_scr.shape[0]

    @pl.when(lax.rem(i, tiles_per_seq) == 0)
    def _():
        tail_scr[...] = jnp.zeros(tail_scr.shape, F32)

    def project(dst, cs):
        dst[...] = jnp.dot(a_ref[...], w_ref[:, cs], preferred_element_type=F32)

    def conv_silu(src, cs):
        cw = cw_ref[:, cs]
        bias = cb_ref[:, cs]
        for c in range(tm // rows):
            r0 = c * rows
            head = tail_scr[:, cs] if c == 0 else src[r0 - halo:r0, :]
            u = src[r0:r0 + rows, :]
            ext = jnp.concatenate([head, u], axis=0)
            y = bias + cw[SSM_CONV_WIDTH - 1:SSM_CONV_WIDTH, :] * u
            for k in range(SSM_CONV_WIDTH - 1):
                shift = SSM_CONV_WIDTH - 1 - k
                y = y + cw[k:k + 1, :] * pltpu.roll(ext, shift, axis=0)[halo:, :]
            o_ref[r0:r0 + rows, cs] = (y * _sigmoid(y)).astype(o_ref.dtype)
        tail_scr[:, cs] = src[tm - halo:, :]

    accs = (acc0, acc1)
    width = acc0.shape[1]
    n_chunks = o_ref.shape[1] // width
    for nc in range(n_chunks):
        cs = slice(nc * width, (nc + 1) * width)
        project(accs[nc % 2], cs)
        if nc > 0:
            conv_silu(accs[(nc - 1) % 2], slice((nc - 1) * width, nc * width))
    conv_silu(accs[(n_chunks - 1) % 2], slice((n_chunks - 1) * width, n_chunks * width))


def conv_projection(a, w, col0, n, conv_w, conv_b, seq, tm=1024, tn=1024, chunk=256):
    m, k = a.shape
    j0 = col0 // tn
    assert j0 * tn == col0
    return pl.pallas_call(
        functools.partial(_conv_proj_kernel, tiles_per_seq=seq // tm),
        grid=(n // tn, m // tm),
        in_specs=[pl.BlockSpec((tm, k), lambda j, i: (i, 0)),
                  pl.BlockSpec((k, tn), lambda j, i: (0, j0 + j)),
                  pl.BlockSpec((SSM_CONV_WIDTH, tn), lambda j, i: (0, j)),
                  pl.BlockSpec((1, tn), lambda j, i: (0, j))],
        out_specs=pl.BlockSpec((tm, tn), lambda j, i: (i, j)),
        out_shape=jax.ShapeDtypeStruct((m, n), BF16),
        scratch_shapes=[pltpu.VMEM((tm, chunk), F32), pltpu.VMEM((tm, chunk), F32),
                        pltpu.VMEM((8, tn), F32)],
        compiler_params=_params("parallel", "arbitrary"),
        name="conv_projection",
    )(a, w, conv_w, conv_b.reshape(1, n))


def _ssd_kernel(xbc_ref, z_ref, dt_ref, a_ref, dskip_ref, nw_ref, expand_ref, o_ref, state_ref,
                *, d_inner):
    c = pl.program_id(1)
    n_groups = SSM_N_GROUPS
    gw = d_inner // n_groups
    pair = 2 * SSM_HEAD_DIM
    lq = SSM_CHUNK

    @pl.when(c == 0)
    def _():
        state_ref[...] = jnp.zeros(state_ref.shape, F32)

    dt = dt_ref[...]
    a = dt * a_ref[...]
    row_i = lax.broadcasted_iota(jnp.int32, (lq, lq), 0)
    col_i = lax.broadcasted_iota(jnp.int32, (lq, lq), 1)
    causal = row_i >= col_i
    tril = jnp.where(causal, 1.0, 0.0).astype(BF16)
    a_hi, a_mid, a_lo = _split3(a)
    acum = (jnp.dot(tril, a_hi, preferred_element_type=F32)
            + jnp.dot(tril, a_mid, preferred_element_type=F32)
            + jnp.dot(tril, a_lo, preferred_element_type=F32))
    acum_t = acum.T
    dt_t = dt.T

    expand = expand_ref[...]

    def widen(v):
        hi, mid, lo = _split3(v)
        return (jnp.dot(hi, expand, preferred_element_type=F32)
                + jnp.dot(mid, expand, preferred_element_type=F32)
                + jnp.dot(lo, expand, preferred_element_type=F32))

    acum_w = widen(acum)
    dt_w = widen(dt)
    alast_w = acum_w[lq - 1:lq, :]
    decay_in_w = jnp.exp(acum_w)
    decay_out_w = jnp.exp(alast_w - acum_w)
    chunk_decay_w = jnp.exp(alast_w)
    lane = lax.broadcasted_iota(jnp.int32, (lq, pair), 1)
    lo_half = lane < SSM_HEAD_DIM

    for g in range(n_groups):
        sl = slice(g * gw, (g + 1) * gw)
        xs = xbc_ref[:, sl]
        bm = xbc_ref[:, d_inner + g * SSM_D_STATE:d_inner + (g + 1) * SSM_D_STATE]
        cm = xbc_ref[:, d_inner + (n_groups + g) * SSM_D_STATE:
                     d_inner + (n_groups + g + 1) * SSM_D_STATE]
        xs32 = xs.astype(F32)
        cb = lax.dot_general(cm, bm, (((1,), (1,)), ((), ())), preferred_element_type=F32)
        state = state_ref[g]
        y = jnp.dot(cm, state.astype(BF16), preferred_element_type=F32) * decay_in_w[:, sl]
        xw = (xs32 * (dt_w[:, sl] * decay_out_w[:, sl])).astype(BF16)
        bm_t = bm.astype(F32).T.astype(BF16)
        state_ref[g] = state * chunk_decay_w[:, sl] + jnp.dot(bm_t, xw, preferred_element_type=F32)

        y_pairs = []
        for hp in range(gw // pair):
            xs_pair = xs[:, hp * pair:(hp + 1) * pair]
            zero = jnp.zeros_like(xs_pair)
            acc = None
            for half in range(2):
                h = (g * gw + hp * pair) // SSM_HEAD_DIM + half
                seg = acum[:, h:h + 1] - acum_t[h:h + 1, :]
                decay = jnp.exp(jnp.where(causal, seg, -jnp.inf))
                mat = (decay * cb * dt_t[h:h + 1, :]).astype(BF16)
                xs_half = jnp.where(lo_half if half == 0 else jnp.logical_not(lo_half), xs_pair, zero)
                part = jnp.dot(mat, xs_half, preferred_element_type=F32)
                acc = part if acc is None else acc + part
            y_pairs.append(acc)
        y = y + jnp.concatenate(y_pairs, axis=1)
        y = y + xs32 * dskip_ref[:, sl]
        zg = z_ref[:, sl].astype(F32)
        y = y * (zg * _sigmoid(zg))
        ms = jnp.mean(y * y, axis=-1, keepdims=True)
        o_ref[:, sl] = (y * lax.rsqrt(ms + RMS_EPS) * nw_ref[:, sl]).astype(o_ref.dtype)


def ssd_mixer_core(xbc, z, dt, a_row, dskip_w, norm_w, expand, bsz, seq):
    t, d_inner = z.shape
    n_chunks = seq // SSM_CHUNK
    gw = d_inner // SSM_N_GROUPS
    row = lambda b, c: (b * n_chunks + c, 0)
    fixed = lambda b, c: (0, 0)
    return pl.pallas_call(
        functools.partial(_ssd_kernel, d_inner=d_inner),
        grid=(bsz, n_chunks),
        in_specs=[pl.BlockSpec((SSM_CHUNK, xbc.shape[1]), row),
                  pl.BlockSpec((SSM_CHUNK, d_inner), row),
                  pl.BlockSpec((SSM_CHUNK, LANES), row),
                  pl.BlockSpec((1, LANES), fixed),
                  pl.BlockSpec((1, d_inner), fixed),
                  pl.BlockSpec((1, d_inner), fixed),
                  pl.BlockSpec((LANES, d_inner), fixed)],
        out_specs=pl.BlockSpec((SSM_CHUNK, d_inner), row),
        out_shape=jax.ShapeDtypeStruct((t, d_inner), BF16),
        scratch_shapes=[pltpu.VMEM((SSM_N_GROUPS, SSM_D_STATE, gw), F32)],
        compiler_params=_params("parallel", "arbitrary"),
        name="ssd_mixer_core",
    )(xbc, z, dt, a_row, dskip_w, norm_w, expand)


def _ffn_kernel(h_ref, x_ref, wg_ref, wu_ref, wd_ref, o_ref):
    @pl.when(pl.program_id(1) == 0)
    def _():
        o_ref[...] = x_ref[...]

    h = h_ref[...]
    gate = jnp.dot(h, wg_ref[...], preferred_element_type=F32)
    up = jnp.dot(h, wu_ref[...], preferred_element_type=F32)
    act = (gate * _sigmoid(gate) * up).astype(BF16)
    o_ref[...] += jnp.dot(act, wd_ref[...], preferred_element_type=F32)


def swiglu_ffn_residual(h, x, w_gate, w_up, w_down, tm=512, tf=512):
    m, d = h.shape
    d_ff = w_gate.shape[1]
    return pl.pallas_call(
        _ffn_kernel,
        grid=(m // tm, d_ff // tf),
        in_specs=[pl.BlockSpec((tm, d), lambda i, f: (i, 0)),
                  pl.BlockSpec((tm, d), lambda i, f: (i, 0)),
                  pl.BlockSpec((d, tf), lambda i, f: (0, f)),
                  pl.BlockSpec((d, tf), lambda i, f: (0, f)),
                  pl.BlockSpec((tf, d), lambda i, f: (f, 0))],
        out_specs=pl.BlockSpec((tm, d), lambda i, f: (i, 0)),
        out_shape=jax.ShapeDtypeStruct((m, d), F32),
        compiler_params=_params("parallel", "arbitrary"),
        name="swiglu_ffn",
    )(h, x, w_gate, w_up, w_down)


def _qk_kernel(a_ref, w_ref, gain_ref, cos_ref, sin_ref, o_ref, *, rows):
    w = w_ref[...]
    gain = gain_ref[...]
    hd = ATTN_HEAD_DIM
    for c in range(a_ref.shape[0] // rows):
        rs = slice(c * rows, (c + 1) * rows)
        acc = jnp.dot(a_ref[rs, :], w, preferred_element_type=F32)
        cos = cos_ref[rs, :]
        sin = sin_ref[rs, :]
        for h in range(acc.shape[1] // hd):
            t = acc[:, h * hd:(h + 1) * hd]
            ms = jnp.mean(t * t, axis=-1, keepdims=True)
            tn = t * lax.rsqrt(ms + RMS_EPS) * gain
            rot = tn * cos + pltpu.roll(tn, hd // 2, axis=1) * sin
            o_ref[rs, h * hd:(h + 1) * hd] = rot.astype(o_ref.dtype)


def qk_projection(a, w, col0, n, gains, cos, sin, seq, tm=1024, rows=256):
    m, k = a.shape
    tiles = seq // tm
    j0 = col0 // n
    assert j0 * n == col0
    return pl.pallas_call(
        functools.partial(_qk_kernel, rows=rows),
        grid=(m // tm, 2),
        in_specs=[pl.BlockSpec((tm, k), lambda i, j: (i, 0)),
                  pl.BlockSpec((k, n), lambda i, j: (0, j0 + j)),
                  pl.BlockSpec((None, 1, ATTN_HEAD_DIM), lambda i, j: (j, 0, 0)),
                  pl.BlockSpec((tm, ATTN_HEAD_DIM), lambda i, j: (i % tiles, 0)),
                  pl.BlockSpec((tm, ATTN_HEAD_DIM), lambda i, j: (i % tiles, 0))],
        out_specs=pl.BlockSpec((None, tm, n), lambda i, j: (j, i, 0)),
        out_shape=jax.ShapeDtypeStruct((2, m, n), BF16),
        compiler_params=_params("parallel", "arbitrary"),
        name="qk_projection",
    )(a, w, gains, cos, sin)


def _attn_kernel(*refs, seq):
    n_groups = len(ATTN_PATTERNS)
    qkv = refs[:3 * n_groups]
    o_ref = refs[3 * n_groups]
    out_scr, lse_scr, vext_all = refs[3 * n_groups + 1:]
    blk = ATTN_BLOCK
    hd = ATTN_HEAD_DIM
    scale = hd ** -0.5
    qi = lax.broadcasted_iota(jnp.int32, (blk, blk), 0)
    kj = lax.broadcasted_iota(jnp.int32, (blk, blk), 1)
    own_ok = kj <= qi
    prev_ok = kj >= qi
    contract_last = (((1,), (1,)), ((), ()))
    for g in range(n_groups):
        vext_all[g, :, hd:] = jnp.ones((seq, hd), BF16)

    def attend(q_ref, k_ref, vext_scr, blocks):
        qs = [q_ref[p0:p0 + blk, :] for p0, _ in blocks]
        s_own = [lax.dot_general(q, k_ref[p0:p0 + blk, :], contract_last, preferred_element_type=F32)
                 for q, (p0, _) in zip(qs, blocks)]
        s_prev = [lax.dot_general(q, k_ref[p0 - blk:p0, :], contract_last, preferred_element_type=F32)
                  if has_prev else None for q, (p0, has_prev) in zip(qs, blocks)]
        results = []
        for so, sp, (p0, has_prev) in zip(s_own, s_prev, blocks):
            so = jnp.where(own_ok, so * scale, -jnp.inf)
            if has_prev:
                sp = jnp.where(prev_ok, sp * scale, -jnp.inf)
                m = jnp.max(jnp.maximum(so, sp), axis=-1, keepdims=True)
            else:
                m = jnp.max(so, axis=-1, keepdims=True)
            ext = jnp.dot(jnp.exp(so - m).astype(BF16), vext_scr[p0:p0 + blk, :],
                          preferred_element_type=F32)
            if has_prev:
                ext = ext + jnp.dot(jnp.exp(sp - m).astype(BF16), vext_scr[p0 - blk:p0, :],
                                    preferred_element_type=F32)
            denom = ext[:, hd:]
            results.append((ext[:, :hd] / denom, m + jnp.log(denom)))
        return results

    batch = 4
    for g, (window, dil) in enumerate(ATTN_PATTERNS):
        assert window // dil == blk
        q_ref, k_ref, v_ref = qkv[3 * g:3 * g + 3]
        vext_scr = vext_all.at[g]
        vext_scr[:, :hd] = v_ref[...]
        nb = seq // dil // blk
        blocks = [((r * nb + i) * blk, i > 0, i * blk * dil + r) for r in range(dil) for i in range(nb)]
        for b0 in range(0, len(blocks), batch):
            chunk = blocks[b0:b0 + batch]
            for (out, lse), (p0, _, t0) in zip(attend(q_ref, k_ref, vext_scr, [c[:2] for c in chunk]), chunk):
                dst = pl.ds(p0, blk) if dil == 1 else pl.ds(t0, blk, stride=dil)
                out_scr[g, dst, :] = out
                lse_scr[g, dst, :] = lse

    lse_all = [lse_scr[g] for g in range(n_groups)]
    top = functools.reduce(jnp.maximum, lse_all)
    num = jnp.zeros((seq, hd), F32)
    den = jnp.zeros((seq, hd), F32)
    for g in range(n_groups):
        wgt = jnp.exp(lse_all[g] - top)
        num = num + wgt * out_scr[g]
        den = den + wgt
    o_ref[...] = (num / den).astype(o_ref.dtype)


def dilated_attention(qk_groups, v_groups, bsz, seq):
    t, n = v_groups[0].shape
    heads = n // ATTN_HEAD_DIM
    in_specs, args = [], []
    for qk, v in zip(qk_groups, v_groups):
        for which in range(2):
            in_specs.append(pl.BlockSpec((None, seq, ATTN_HEAD_DIM),
                                         lambda b, h, which=which: (which, b, h)))
            args.append(qk)
        in_specs.append(pl.BlockSpec((seq, ATTN_HEAD_DIM), lambda b, h: (b, h)))
        args.append(v)
    n_groups = len(v_groups)
    return pl.pallas_call(
        functools.partial(_attn_kernel, seq=seq),
        grid=(bsz, heads),
        in_specs=in_specs,
        out_specs=pl.BlockSpec((seq, ATTN_HEAD_DIM), lambda b, h: (b, h)),
        out_shape=jax.ShapeDtypeStruct((t, n), BF16),
        scratch_shapes=[pltpu.VMEM((n_groups, seq, ATTN_HEAD_DIM), F32),
                        pltpu.VMEM((n_groups, seq, ATTN_HEAD_DIM), F32),
                        pltpu.VMEM((n_groups, seq, 2 * ATTN_HEAD_DIM), BF16)],
        compiler_params=_params("parallel", "parallel"),
        name="dilated_attention",
    )(*args)


def _dilate_rows(table, dil):
    seq, c = table.shape
    return table.reshape(seq // dil, dil, c).transpose(1, 0, 2).reshape(seq, c)


def _rope_tables(seq):
    half = ATTN_HEAD_DIM // 2
    inv_freq = ROPE_THETA ** (-jnp.arange(half, dtype=F32) / half)
    ang = jnp.arange(seq, dtype=F32)[:, None] * inv_freq[None, :]
    cos, sin = jnp.cos(ang), jnp.sin(ang)
    return jnp.concatenate([cos, cos], axis=1), jnp.concatenate([-sin, sin], axis=1)


def mamba2_layer(x, h, bsz, seq, w_in, conv_w, conv_b, dt_bias, a_log, d_skip, norm_w, w_out):
    d_inner = w_out.shape[0]
    n_heads = d_inner // SSM_HEAD_DIM
    conv_dim = conv_w.shape[1]
    d_model = w_in.shape[0]
    w_in_bf = w_in.astype(BF16)
    w_dt = jnp.zeros((d_model, LANES), BF16).at[:, :n_heads].set(w_in_bf[:, d_inner + conv_dim:])
    dt_bias_row = jnp.zeros((1, LANES), F32).at[0, :n_heads].set(dt_bias.astype(F32))
    a_row = jnp.zeros((1, LANES), F32).at[0, :n_heads].set(-jnp.exp(a_log.astype(F32)))
    dskip_w = jnp.repeat(d_skip.astype(F32), SSM_HEAD_DIM).reshape(1, d_inner)
    head_of_channel = jnp.arange(d_inner) // SSM_HEAD_DIM
    expand = (jnp.arange(LANES)[:, None] == head_of_channel[None, :]).astype(BF16)

    z = matmul(h, w_in_bf, BF16, col0=0, n=d_inner)
    xbc = conv_projection(h, w_in_bf, d_inner, conv_dim, conv_w.astype(F32), conv_b.astype(F32), seq)
    dt = dt_projection(h, w_dt, dt_bias_row)
    y = ssd_mixer_core(xbc, z, dt, a_row, dskip_w, norm_w.astype(F32).reshape(1, d_inner), expand,
                       bsz, seq)
    return matmul_residual(y, w_out.astype(BF16), x)


def attention_layer(x, hs, bsz, seq, w_qkv, q_norm, k_norm, w_out):
    d_model = w_qkv.shape[0]
    n_groups = len(ATTN_PATTERNS)
    n = ATTN_GROUP_HEADS * ATTN_HEAD_DIM
    w = w_qkv.astype(BF16)
    cos, sin = _rope_tables(seq)
    qk_groups, v_groups = [], []
    for g, (_, dil) in enumerate(ATTN_PATTERNS):
        gains = jnp.stack([q_norm[g], k_norm[g]]).astype(F32).reshape(2, 1, ATTN_HEAD_DIM)
        col0 = g * 3 * n
        qk_groups.append(qk_projection(hs[g], w, col0, n, gains, _dilate_rows(cos, dil),
                                       _dilate_rows(sin, dil), seq))
        v_groups.append(matmul(hs[g], w, BF16, col0=col0 + 2 * n, n=n))
    merged = dilated_attention(qk_groups, v_groups, bsz, seq)
    return matmul_residual(merged, w_out.astype(BF16), x)


def kernel(x, mix_norm, ffn_norm, ssm_w_in, ssm_conv_w, ssm_conv_b, ssm_dt_bias, ssm_a_log, ssm_d, ssm_norm, ssm_w_out, attn_w_qkv, attn_q_norm, attn_k_norm, attn_w_out, ffn_w_gate, ffn_w_up, ffn_w_down):
    bsz, seq, d_model = x.shape
    depth = mix_norm.shape[0]
    xt = x.reshape(bsz * seq, d_model).astype(F32)
    for layer in range(depth):
        j = layer // 2
        if layer % 2 == 0:
            h = rmsnorm(xt, mix_norm[layer].astype(F32))
            xt = mamba2_layer(xt, h, bsz, seq, ssm_w_in[j], ssm_conv_w[j], ssm_conv_b[j], ssm_dt_bias[j],
                              ssm_a_log[j], ssm_d[j], ssm_norm[j], ssm_w_out[j])
        else:
            hs = rmsnorm_dilated(xt, mix_norm[layer].astype(F32), bsz, seq)
            xt = attention_layer(xt, hs, bsz, seq, attn_w_qkv[j], attn_q_norm[j], attn_k_norm[j],
                                 attn_w_out[j])
        h = rmsnorm(xt, ffn_norm[layer].astype(F32))
        xt = swiglu_ffn_residual(h, xt, ffn_w_gate[layer].astype(BF16), ffn_w_up[layer].astype(BF16),
                                 ffn_w_down[layer].astype(BF16))
    return xt.reshape(bsz, seq, d_model).astype(x.dtype)
```

```python
import functools
import math

import jax
import jax.numpy as jnp
from jax import lax
from jax.experimental import pallas as pl
from jax.experimental.pallas import tpu as pltpu

F32 = jnp.float32
BF16 = jnp.bfloat16

RMS_EPS = 1e-6
LANES = 128
VMEM_LIMIT_BYTES = 56 * 1024 * 1024

SSM_HEAD_DIM = 64
SSM_D_STATE = 128
SSM_N_GROUPS = 8
SSM_CONV_WIDTH = 4
SSM_CHUNK = 128

ATTN_HEAD_DIM = 128
ATTN_GROUP_HEADS = 8
ATTN_PATTERNS = ((128, 1), (512, 4), (2048, 16))
ATTN_BLOCK = 128
ROPE_THETA = 10000.0


def _params(*sem):
    return pltpu.CompilerParams(dimension_semantics=sem, vmem_limit_bytes=VMEM_LIMIT_BYTES)


def _sigmoid(v):
    return 1.0 / (1.0 + jnp.exp(-v))


def _split3(v):
    hi = v.astype(BF16)
    r1 = v - hi.astype(F32)
    mid = r1.astype(BF16)
    lo = (r1 - mid.astype(F32)).astype(BF16)
    return hi, mid, lo


def _rmsnorm_kernel(x_ref, g_ref, o_ref):
    x = x_ref[...]
    ms = jnp.mean(x * x, axis=-1, keepdims=True)
    o_ref[...] = (x * lax.rsqrt(ms + RMS_EPS) * g_ref[...]).astype(o_ref.dtype)


def rmsnorm(x, gain, tm=512):
    t, d = x.shape
    return pl.pallas_call(
        _rmsnorm_kernel,
        grid=(t // tm,),
        in_specs=[pl.BlockSpec((tm, d), lambda i: (i, 0)),
                  pl.BlockSpec((1, d), lambda i: (0, 0))],
        out_specs=pl.BlockSpec((tm, d), lambda i: (i, 0)),
        out_shape=jax.ShapeDtypeStruct((t, d), BF16),
        compiler_params=_params("parallel"),
        name="rmsnorm",
    )(x, gain.reshape(1, d))


def _rmsnorm_dilated_kernel(x_ref, g_ref, o1_ref, o4_ref, o16_ref, scr):
    x = x_ref[...]
    ms = jnp.mean(x * x, axis=-1, keepdims=True)
    y = x * lax.rsqrt(ms + RMS_EPS) * g_ref[...]
    o1_ref[...] = y.astype(o1_ref.dtype)
    n_slabs = scr.shape[0]
    for s in range(n_slabs):
        scr[s] = y[:, s * LANES:(s + 1) * LANES]
    for o_ref in (o4_ref, o16_ref):
        dil, rows = o_ref.shape[0], o_ref.shape[1]
        for r in range(dil):
            for s in range(n_slabs):
                o_ref[r, :, s * LANES:(s + 1) * LANES] = (
                    scr[s, pl.ds(r, rows, stride=dil), :].astype(o_ref.dtype))


def rmsnorm_dilated(x, gain, bsz, seq, tm=256):
    t, d = x.shape
    tiles = seq // tm
    d4, d16 = ATTN_PATTERNS[1][1], ATTN_PATTERNS[2][1]
    o1, o4, o16 = pl.pallas_call(
        _rmsnorm_dilated_kernel,
        grid=(bsz, tiles),
        in_specs=[pl.BlockSpec((tm, d), lambda b, i: (b * tiles + i, 0)),
                  pl.BlockSpec((1, d), lambda b, i: (0, 0))],
        out_specs=[pl.BlockSpec((tm, d), lambda b, i: (b * tiles + i, 0)),
                   pl.BlockSpec((None, d4, tm // d4, d), lambda b, i: (b, 0, i, 0)),
                   pl.BlockSpec((None, d16, tm // d16, d), lambda b, i: (b, 0, i, 0))],
        out_shape=[jax.ShapeDtypeStruct((t, d), BF16),
                   jax.ShapeDtypeStruct((bsz, d4, seq // d4, d), BF16),
                   jax.ShapeDtypeStruct((bsz, d16, seq // d16, d), BF16)],
        scratch_shapes=[pltpu.VMEM((d // LANES, tm, LANES), F32)],
        compiler_params=_params("parallel", "parallel"),
        name="rmsnorm_dilated",
    )(x, gain.reshape(1, d))
    return o1, o4.reshape(t, d), o16.reshape(t, d)


def _mm_kernel(a_ref, w_ref, o_ref):
    o_ref[...] = jnp.dot(a_ref[...], w_ref[...], preferred_element_type=F32).astype(o_ref.dtype)


def matmul(a, w, out_dtype, col0=0, n=None, tm=1024, tn=512):
    m, k = a.shape
    n = w.shape[1] if n is None else n
    j0 = col0 // tn
    assert j0 * tn == col0
    return pl.pallas_call(
        _mm_kernel,
        grid=(m // tm, n // tn),
        in_specs=[pl.BlockSpec((tm, k), lambda i, j: (i, 0)),
                  pl.BlockSpec((k, tn), lambda i, j: (0, j0 + j))],
        out_specs=pl.BlockSpec((tm, tn), lambda i, j: (i, j)),
        out_shape=jax.ShapeDtypeStruct((m, n), out_dtype),
        compiler_params=_params("parallel", "parallel"),
        name="matmul",
    )(a, w)


def _mm_res_kernel(a_ref, w_ref, r_ref, o_ref):
    o_ref[...] = r_ref[...] + jnp.dot(a_ref[...], w_ref[...], preferred_element_type=F32)


def matmul_residual(a, w, res, tm=1024, tn=512):
    m, k = a.shape
    n = w.shape[1]
    return pl.pallas_call(
        _mm_res_kernel,
        grid=(m // tm, n // tn),
        in_specs=[pl.BlockSpec((tm, k), lambda i, j: (i, 0)),
                  pl.BlockSpec((k, tn), lambda i, j: (0, j)),
                  pl.BlockSpec((tm, tn), lambda i, j: (i, j))],
        out_specs=pl.BlockSpec((tm, tn), lambda i, j: (i, j)),
        out_shape=jax.ShapeDtypeStruct((m, n), F32),
        compiler_params=_params("parallel", "parallel"),
        name="matmul_residual",
    )(a, w, res)


def _dt_kernel(a_ref, w_ref, b_ref, o_ref):
    raw = jnp.dot(a_ref[...], w_ref[...], preferred_element_type=F32) + b_ref[...]
    o_ref[...] = jnp.maximum(raw, 0.0) + jnp.log(1.0 + jnp.exp(-jnp.abs(raw)))


def dt_projection(a, w, bias, tm=1024):
    m, k = a.shape
    n = w.shape[1]
    return pl.pallas_call(
        _dt_kernel,
        grid=(m // tm,),
        in_specs=[pl.BlockSpec((tm, k), lambda i: (i, 0)),
                  pl.BlockSpec((k, n), lambda i: (0, 0)),
                  pl.BlockSpec((1, n), lambda i: (0, 0))],
        out_specs=pl.BlockSpec((tm, n), lambda i: (i, 0)),
        out_shape=jax.ShapeDtypeStruct((m, n), F32),
        compiler_params=_params("parallel"),
        name="dt_projection",
    )(a, w, bias)


def _conv_proj_kernel(a_ref, w_ref, cw_ref, cb_ref, o_ref, acc0, acc1, tail_scr, *, tiles_per_seq, rows=64):
    i = pl.program_id(1)
    tm = a_ref.shape[0]
    halo = tail_scr.shape[0]

    @pl.when(lax.rem(i, tiles_per_seq) == 0)
    def _():
        tail_scr[...] = jnp.zeros(tail_scr.shape, F32)

    def project(dst, cs):
        dst[...] = jnp.dot(a_ref[...], w_ref[:, cs], preferred_element_type=F32)

    def conv_silu(src, cs):
        cw = cw_ref[:, cs]
        bias = cb_ref[:, cs]
        for c in range(tm // rows):
            r0 = c * rows
            head = tail_scr[:, cs] if c == 0 else src[r0 - halo:r0, :]
            u = src[r0:r0 + rows, :]
            ext = jnp.concatenate([head, u], axis=0)
            y = bias + cw[SSM_CONV_WIDTH - 1:SSM_CONV_WIDTH, :] * u
            for k in range(SSM_CONV_WIDTH - 1):
                shift = SSM_CONV_WIDTH - 1 - k
                y = y + cw[k:k + 1, :] * pltpu.roll(ext, shift, axis=0)[halo:, :]
            o_ref[r0:r0 + rows, cs] = (y * _sigmoid(y)).astype(o_ref.dtype)
        tail_scr[:, cs] = src[tm - halo:, :]

    accs = (acc0, acc1)
    width = acc0.shape[1]
    n_chunks = o_ref.shape[1] // width
    for nc in range(n_chunks):
        cs = slice(nc * width, (nc + 1) * width)
        project(accs[nc % 2], cs)
        if nc > 0:
            conv_silu(accs[(nc - 1) % 2], slice((nc - 1) * width, nc * width))
    conv_silu(accs[(n_chunks - 1) % 2], slice((n_chunks - 1) * width, n_chunks * width))


def conv_projection(a, w, col0, n, conv_w, conv_b, seq, tm=1024, tn=1024, chunk=256):
    m, k = a.shape
    j0 = col0 // tn
    assert j0 * tn == col0
    return pl.pallas_call(
        functools.partial(_conv_proj_kernel, tiles_per_seq=seq // tm),
        grid=(n // tn, m // tm),
        in_specs=[pl.BlockSpec((tm, k), lambda j, i: (i, 0)),
                  pl.BlockSpec((k, tn), lambda j, i: (0, j0 + j)),
                  pl.BlockSpec((SSM_CONV_WIDTH, tn), lambda j, i: (0, j)),
                  pl.BlockSpec((1, tn), lambda j, i: (0, j))],
        out_specs=pl.BlockSpec((tm, tn), lambda j, i: (i, j)),
        out_shape=jax.ShapeDtypeStruct((m, n), BF16),
        scratch_shapes=[pltpu.VMEM((tm, chunk), F32), pltpu.VMEM((tm, chunk), F32),
                        pltpu.VMEM((8, tn), F32)],
        compiler_params=_params("parallel", "arbitrary"),
        name="conv_projection",
    )(a, w, conv_w, conv_b.reshape(1, n))


def _ssd_kernel(xbc_ref, z_ref, dt_ref, a_ref, dskip_ref, nw_ref, expand_ref, o_ref, state_ref,
                *, d_inner):
    c = pl.program_id(1)
    n_groups = SSM_N_GROUPS
    gw = d_inner // n_groups
    pair = 2 * SSM_HEAD_DIM
    lq = SSM_CHUNK

    @pl.when(c == 0)
    def _():
        state_ref[...] = jnp.zeros(state_ref.shape, F32)

    dt = dt_ref[...]
    a = dt * a_ref[...]
    row_i = lax.broadcasted_iota(jnp.int32, (lq, lq), 0)
    col_i = lax.broadcasted_iota(jnp.int32, (lq, lq), 1)
    causal = row_i >= col_i
    tril = jnp.where(causal, 1.0, 0.0).astype(BF16)
    tril3 = jnp.concatenate([tril, tril, tril], axis=1)
    acum = jnp.dot(tril3, jnp.concatenate(_split3(a), axis=0),
                   preferred_element_type=F32)
    acum_t = acum.T
    dt_t = dt.T
    alast = acum[lq - 1:lq, :]
    stacked = jnp.concatenate([
        jnp.exp(acum),
        dt * jnp.exp(alast - acum),
        jnp.broadcast_to(jnp.exp(alast), (8, LANES)),
    ], axis=0)
    wide = jnp.dot(jnp.concatenate(_split3(stacked), axis=1), expand_ref[...],
                   preferred_element_type=F32)
    decay_in_w = wide[0:lq, :]
    dt_decay_out_w = wide[lq:2 * lq, :]
    chunk_decay_w = wide[2 * lq:2 * lq + 1, :]
    lane = lax.broadcasted_iota(jnp.int32, (lq, pair), 1)
    lo_half = lane < SSM_HEAD_DIM

    for g in range(n_groups):
        sl = slice(g * gw, (g + 1) * gw)
        xs = xbc_ref[:, sl]
        bm = xbc_ref[:, d_inner + g * SSM_D_STATE:d_inner + (g + 1) * SSM_D_STATE]
        cm = xbc_ref[:, d_inner + (n_groups + g) * SSM_D_STATE:
                     d_inner + (n_groups + g + 1) * SSM_D_STATE]
        xs32 = xs.astype(F32)
        cb = lax.dot_general(cm, bm, (((1,), (1,)), ((), ())), preferred_element_type=F32)
        state = state_ref[g]
        y = jnp.dot(cm, state.astype(BF16), preferred_element_type=F32) * decay_in_w[:, sl]
        xw = xs * dt_decay_out_w[:, sl].astype(BF16)
        bm_t = bm.astype(F32).T.astype(BF16)
        state_ref[g] = state * chunk_decay_w[:, sl] + jnp.dot(bm_t, xw, preferred_element_type=F32)

        y_pairs = []
        for hp in range(gw // pair):
            xs_pair = xs[:, hp * pair:(hp + 1) * pair]
            zero = jnp.zeros_like(xs_pair)
            mats = []
            for half in range(2):
                h = (g * gw + hp * pair) // SSM_HEAD_DIM + half
                seg = acum[:, h:h + 1] - acum_t[h:h + 1, :]
                decay = jnp.exp(jnp.where(causal, seg, -jnp.inf))
                mats.append((decay * cb * dt_t[h:h + 1, :]).astype(BF16))
            rhs = jnp.concatenate([jnp.where(lo_half, xs_pair, zero),
                                   jnp.where(lo_half, zero, xs_pair)], axis=0)
            y_pairs.append(jnp.dot(jnp.concatenate(mats, axis=1), rhs, preferred_element_type=F32))
        y = y + jnp.concatenate(y_pairs, axis=1)
        y = y + xs32 * dskip_ref[:, sl]
        zg = z_ref[:, sl].astype(F32)
        y = y * (zg * _sigmoid(zg))
        ms = jnp.mean(y * y, axis=-1, keepdims=True)
        o_ref[:, sl] = (y * lax.rsqrt(ms + RMS_EPS) * nw_ref[:, sl]).astype(o_ref.dtype)


def ssd_mixer_core(xbc, z, dt, a_row, dskip_w, norm_w, expand, bsz, seq):
    t, d_inner = z.shape
    n_chunks = seq // SSM_CHUNK
    gw = d_inner // SSM_N_GROUPS
    row = lambda b, c: (b * n_chunks + c, 0)
    fixed = lambda b, c: (0, 0)
    return pl.pallas_call(
        functools.partial(_ssd_kernel, d_inner=d_inner),
        grid=(bsz, n_chunks),
        in_specs=[pl.BlockSpec((SSM_CHUNK, xbc.shape[1]), row),
                  pl.BlockSpec((SSM_CHUNK, d_inner), row),
                  pl.BlockSpec((SSM_CHUNK, LANES), row),
                  pl.BlockSpec((1, LANES), fixed),
                  pl.BlockSpec((1, d_inner), fixed),
                  pl.BlockSpec((1, d_inner), fixed),
                  pl.BlockSpec((3 * LANES, d_inner), fixed)],
        out_specs=pl.BlockSpec((SSM_CHUNK, d_inner), row),
        out_shape=jax.ShapeDtypeStruct((t, d_inner), BF16),
        scratch_shapes=[pltpu.VMEM((SSM_N_GROUPS, SSM_D_STATE, gw), F32)],
        compiler_params=_params("parallel", "arbitrary"),
        name="ssd_mixer_core",
    )(xbc, z, dt, a_row, dskip_w, norm_w, expand)


def _ffn_kernel(h_ref, x_ref, wg_ref, wu_ref, wd_ref, o_ref, act0, act1, *, n_f):
    f = pl.program_id(1)

    def activate(dst):
        h = h_ref[...]
        gate = jnp.dot(h, wg_ref[...], preferred_element_type=F32)
        up = jnp.dot(h, wu_ref[...], preferred_element_type=F32)
        dst[...] = (gate * _sigmoid(gate) * up).astype(BF16)

    def project(src):
        o_ref[...] += jnp.dot(src[...], wd_ref[...], preferred_element_type=F32)

    even = lax.rem(f, 2) == 0

    @pl.when(f == 0)
    def _():
        o_ref[...] = x_ref[...]
        activate(act0)

    @pl.when(jnp.logical_and(jnp.logical_and(f > 0, f < n_f), even))
    def _():
        activate(act0)
        project(act1)

    @pl.when(jnp.logical_and(f < n_f, jnp.logical_not(even)))
    def _():
        activate(act1)
        project(act0)

    @pl.when(f == n_f)
    def _():
        project(act1 if n_f % 2 == 0 else act0)


def swiglu_ffn_residual(h, x, w_gate, w_up, w_down, tm=512, tf=512):
    m, d = h.shape
    d_ff = w_gate.shape[1]
    n_f = d_ff // tf
    return pl.pallas_call(
        functools.partial(_ffn_kernel, n_f=n_f),
        grid=(m // tm, n_f + 1),
        in_specs=[pl.BlockSpec((tm, d), lambda i, f: (i, 0)),
                  pl.BlockSpec((tm, d), lambda i, f: (i, 0)),
                  pl.BlockSpec((d, tf), lambda i, f: (0, jnp.minimum(f, n_f - 1))),
                  pl.BlockSpec((d, tf), lambda i, f: (0, jnp.minimum(f, n_f - 1))),
                  pl.BlockSpec((tf, d), lambda i, f: (jnp.maximum(f - 1, 0), 0))],
        out_specs=pl.BlockSpec((tm, d), lambda i, f: (i, 0)),
        out_shape=jax.ShapeDtypeStruct((m, d), F32),
        scratch_shapes=[pltpu.VMEM((tm, tf), BF16), pltpu.VMEM((tm, tf), BF16)],
        compiler_params=_params("parallel", "arbitrary"),
        name="swiglu_ffn",
    )(h, x, w_gate, w_up, w_down)


def _qk_kernel(a_ref, w_ref, gain_ref, cos_ref, sin_ref, o_ref, *, rows):
    w = w_ref[...]
    gain = gain_ref[...]
    hd = ATTN_HEAD_DIM
    for c in range(a_ref.shape[0] // rows):
        rs = slice(c * rows, (c + 1) * rows)
        acc = jnp.dot(a_ref[rs, :], w, preferred_element_type=F32)
        cos = cos_ref[rs, :]
        sin = sin_ref[rs, :]
        for h in range(acc.shape[1] // hd):
            t = acc[:, h * hd:(h + 1) * hd]
            ms = jnp.mean(t * t, axis=-1, keepdims=True)
            tn = t * lax.rsqrt(ms + RMS_EPS) * gain
            rot = tn * cos + pltpu.roll(tn, hd // 2, axis=1) * sin
            o_ref[rs, h * hd:(h + 1) * hd] = rot.astype(o_ref.dtype)


def qk_projection(a, w, col0, n, gains, cos, sin, seq, tm=1024, rows=256):
    m, k = a.shape
    tiles = seq // tm
    j0 = col0 // n
    assert j0 * n == col0
    return pl.pallas_call(
        functools.partial(_qk_kernel, rows=rows),
        grid=(m // tm, 2),
        in_specs=[pl.BlockSpec((tm, k), lambda i, j: (i, 0)),
                  pl.BlockSpec((k, n), lambda i, j: (0, j0 + j)),
                  pl.BlockSpec((None, 1, ATTN_HEAD_DIM), lambda i, j: (j, 0, 0)),
                  pl.BlockSpec((tm, ATTN_HEAD_DIM), lambda i, j: (i % tiles, 0)),
                  pl.BlockSpec((tm, ATTN_HEAD_DIM), lambda i, j: (i % tiles, 0))],
        out_specs=pl.BlockSpec((None, tm, n), lambda i, j: (j, i, 0)),
        out_shape=jax.ShapeDtypeStruct((2, m, n), BF16),
        compiler_params=_params("parallel", "arbitrary"),
        name="qk_projection",
    )(a, w, gains, cos, sin)


def _attn_kernel(*refs, seq):
    n_groups = len(ATTN_PATTERNS)
    qkv = refs[:3 * n_groups]
    o_ref = refs[3 * n_groups]
    out_scr, lse_scr, vext_all = refs[3 * n_groups + 1:]
    blk = ATTN_BLOCK
    hd = ATTN_HEAD_DIM
    scale = hd ** -0.5
    qi = lax.broadcasted_iota(jnp.int32, (blk, blk), 0)
    kj = lax.broadcasted_iota(jnp.int32, (blk, blk), 1)
    own_ok = kj <= qi
    prev_ok = kj >= qi
    contract_last = (((1,), (1,)), ((), ()))
    for g in range(n_groups):
        vext_all[g, :, hd:] = jnp.ones((seq, hd), BF16)

    def attend(q_ref, k_ref, vext_scr, blocks):
        qs = [q_ref[p0:p0 + blk, :] for p0, _ in blocks]
        s_own = [lax.dot_general(q, k_ref[p0:p0 + blk, :], contract_last, preferred_element_type=F32)
                 for q, (p0, _) in zip(qs, blocks)]
        s_prev = [lax.dot_general(q, k_ref[p0 - blk:p0, :], contract_last, preferred_element_type=F32)
                  if has_prev else None for q, (p0, has_prev) in zip(qs, blocks)]
        results = []
        for so, sp, (p0, has_prev) in zip(s_own, s_prev, blocks):
            so = jnp.where(own_ok, so * scale, -jnp.inf)
            if has_prev:
                sp = jnp.where(prev_ok, sp * scale, -jnp.inf)
                m = jnp.max(jnp.maximum(so, sp), axis=-1, keepdims=True)
            else:
                m = jnp.max(so, axis=-1, keepdims=True)
            ext = jnp.dot(jnp.exp(so - m).astype(BF16), vext_scr[p0:p0 + blk, :],
                          preferred_element_type=F32)
            if has_prev:
                ext = ext + jnp.dot(jnp.exp(sp - m).astype(BF16), vext_scr[p0 - blk:p0, :],
                                    preferred_element_type=F32)
            denom = ext[:, hd:]
            results.append((ext[:, :hd] / denom, m + jnp.log(denom)))
        return results

    batch = 4
    for g, (window, dil) in enumerate(ATTN_PATTERNS):
        assert window // dil == blk
        q_ref, k_ref, v_ref = qkv[3 * g:3 * g + 3]
        vext_scr = vext_all.at[g]
        vext_scr[:, :hd] = v_ref[...]
        nb = seq // dil // blk
        blocks = [((r * nb + i) * blk, i > 0, i * blk * dil + r) for r in range(dil) for i in range(nb)]
        for b0 in range(0, len(blocks), batch):
            chunk = blocks[b0:b0 + batch]
            for (out, lse), (p0, _, t0) in zip(attend(q_ref, k_ref, vext_scr, [c[:2] for c in chunk]), chunk):
                dst = pl.ds(p0, blk) if dil == 1 else pl.ds(t0, blk, stride=dil)
                out_scr[g, dst, :] = out
                lse_scr[g, dst, :] = lse

    lse_all = [lse_scr[g] for g in range(n_groups)]
    top = functools.reduce(jnp.maximum, lse_all)
    num = jnp.zeros((seq, hd), F32)
    den = jnp.zeros((seq, hd), F32)
    for g in range(n_groups):
        wgt = jnp.exp(lse_all[g] - top)
        num = num + wgt * out_scr[g]
        den = den + wgt
    o_ref[...] = (num / den).astype(o_ref.dtype)


def dilated_attention(qk_groups, v_groups, bsz, seq):
    t, n = v_groups[0].shape
    heads = n // ATTN_HEAD_DIM
    in_specs, args = [], []
    for qk, v in zip(qk_groups, v_groups):
        for which in range(2):
            in_specs.append(pl.BlockSpec((None, seq, ATTN_HEAD_DIM),
                                         lambda b, h, which=which: (which, b, h)))
            args.append(qk)
        in_specs.append(pl.BlockSpec((seq, ATTN_HEAD_DIM), lambda b, h: (b, h)))
        args.append(v)
    n_groups = len(v_groups)
    return pl.pallas_call(
        functools.partial(_attn_kernel, seq=seq),
        grid=(bsz, heads),
        in_specs=in_specs,
        out_specs=pl.BlockSpec((seq, ATTN_HEAD_DIM), lambda b, h: (b, h)),
        out_shape=jax.ShapeDtypeStruct((t, n), BF16),
        scratch_shapes=[pltpu.VMEM((n_groups, seq, ATTN_HEAD_DIM), F32),
                        pltpu.VMEM((n_groups, seq, ATTN_HEAD_DIM), F32),
                        pltpu.VMEM((n_groups, seq, 2 * ATTN_HEAD_DIM), BF16)],
        compiler_params=_params("parallel", "parallel"),
        name="dilated_attention",
    )(*args)


def _dilate_rows(table, dil):
    seq, c = table.shape
    return table.reshape(seq // dil, dil, c).transpose(1, 0, 2).reshape(seq, c)


def _rope_tables(seq):
    half = ATTN_HEAD_DIM // 2
    inv_freq = ROPE_THETA ** (-jnp.arange(half, dtype=F32) / half)
    ang = jnp.arange(seq, dtype=F32)[:, None] * inv_freq[None, :]
    cos, sin = jnp.cos(ang), jnp.sin(ang)
    return jnp.concatenate([cos, cos], axis=1), jnp.concatenate([-sin, sin], axis=1)


def mamba2_layer(x, h, bsz, seq, w_in, conv_w, conv_b, dt_bias, a_log, d_skip, norm_w, w_out):
    d_inner = w_out.shape[0]
    n_heads = d_inner // SSM_HEAD_DIM
    conv_dim = conv_w.shape[1]
    d_model = w_in.shape[0]
    w_in_bf = w_in.astype(BF16)
    w_dt = jnp.zeros((d_model, LANES), BF16).at[:, :n_heads].set(w_in_bf[:, d_inner + conv_dim:])
    dt_bias_row = jnp.zeros((1, LANES), F32).at[0, :n_heads].set(dt_bias.astype(F32))
    a_row = jnp.zeros((1, LANES), F32).at[0, :n_heads].set(-jnp.exp(a_log.astype(F32)))
    dskip_w = jnp.repeat(d_skip.astype(F32), SSM_HEAD_DIM).reshape(1, d_inner)
    head_of_channel = jnp.arange(d_inner) // SSM_HEAD_DIM
    expand = (jnp.arange(LANES)[:, None] == head_of_channel[None, :]).astype(BF16)
    expand = jnp.concatenate([expand, expand, expand], axis=0)

    z = matmul(h, w_in_bf, BF16, col0=0, n=d_inner)
    xbc = conv_projection(h, w_in_bf, d_inner, conv_dim, conv_w.astype(F32), conv_b.astype(F32), seq)
    dt = dt_projection(h, w_dt, dt_bias_row)
    y = ssd_mixer_core(xbc, z, dt, a_row, dskip_w, norm_w.astype(F32).reshape(1, d_inner), expand,
                       bsz, seq)
    return matmul_residual(y, w_out.astype(BF16), x)


def attention_layer(x, hs, bsz, seq, w_qkv, q_norm, k_norm, w_out):
    d_model = w_qkv.shape[0]
    n_groups = len(ATTN_PATTERNS)
    n = ATTN_GROUP_HEADS * ATTN_HEAD_DIM
    w = w_qkv.astype(BF16)
    cos, sin = _rope_tables(seq)
    qk_groups, v_groups = [], []
    for g, (_, dil) in enumerate(ATTN_PATTERNS):
        gains = jnp.stack([q_norm[g], k_norm[g]]).astype(F32).reshape(2, 1, ATTN_HEAD_DIM)
        col0 = g * 3 * n
        qk_groups.append(qk_projection(hs[g], w, col0, n, gains, _dilate_rows(cos, dil),
                                       _dilate_rows(sin, dil), seq))
        v_groups.append(matmul(hs[g], w, BF16, col0=col0 + 2 * n, n=n))
    merged = dilated_attention(qk_groups, v_groups, bsz, seq)
    return matmul_residual(merged, w_out.astype(BF16), x)


def kernel(x, mix_norm, ffn_norm, ssm_w_in, ssm_conv_w, ssm_conv_b, ssm_dt_bias, ssm_a_log, ssm_d, ssm_norm, ssm_w_out, attn_w_qkv, attn_q_norm, attn_k_norm, attn_w_out, ffn_w_gate, ffn_w_up, ffn_w_down):
    bsz, seq, d_model = x.shape
    depth = mix_norm.shape[0]
    xt = x.reshape(bsz * seq, d_model).astype(F32)
    for layer in range(depth):
        j = layer // 2
        if layer % 2 == 0:
            h = rmsnorm(xt, mix_norm[layer].astype(F32))
            xt = mamba2_layer(xt, h, bsz, seq, ssm_w_in[j], ssm_conv_w[j], ssm_conv_b[j], ssm_dt_bias[j],
                              ssm_a_log[j], ssm_d[j], ssm_norm[j], ssm_w_out[j])
        else:
            hs = rmsnorm_dilated(xt, mix_norm[layer].astype(F32), bsz, seq)
            xt = attention_layer(xt, hs, bsz, seq, attn_w_qkv[j], attn_q_norm[j], attn_k_norm[j],
                                 attn_w_out[j])
        h = rmsnorm(xt, ffn_norm[layer].astype(F32))
        xt = swiglu_ffn_residual(h, xt, ffn_w_gate[layer].astype(BF16), ffn_w_up[layer].astype(BF16),
                                 ffn_w_down[layer].astype(BF16))
    return xt.reshape(bsz, seq, d_model).astype(x.dtype)
```

```python
import functools
import math

import jax
import jax.numpy as jnp
from jax import lax
from jax.experimental import pallas as pl
from jax.experimental.pallas import tpu as pltpu

F32 = jnp.float32
BF16 = jnp.bfloat16

RMS_EPS = 1e-6
LANES = 128
VMEM_LIMIT_BYTES = 56 * 1024 * 1024

SSM_HEAD_DIM = 64
SSM_D_STATE = 128
SSM_N_GROUPS = 8
SSM_CONV_WIDTH = 4
SSM_CHUNK = 128

ATTN_HEAD_DIM = 128
ATTN_GROUP_HEADS = 8
ATTN_PATTERNS = ((128, 1), (512, 4), (2048, 16))
ATTN_BLOCK = 128
ROPE_THETA = 10000.0


def _params(*sem):
    return pltpu.CompilerParams(dimension_semantics=sem, vmem_limit_bytes=VMEM_LIMIT_BYTES)


def _sigmoid(v):
    return 1.0 / (1.0 + jnp.exp(-v))


def _split3(v):
    hi = v.astype(BF16)
    r1 = v - hi.astype(F32)
    mid = r1.astype(BF16)
    lo = (r1 - mid.astype(F32)).astype(BF16)
    return hi, mid, lo


def _rmsnorm_kernel(x_ref, g_ref, o_ref):
    x = x_ref[...]
    ms = jnp.mean(x * x, axis=-1, keepdims=True)
    o_ref[...] = (x * lax.rsqrt(ms + RMS_EPS) * g_ref[...]).astype(o_ref.dtype)


def rmsnorm(x, gain, tm=512):
    t, d = x.shape
    return pl.pallas_call(
        _rmsnorm_kernel,
        grid=(t // tm,),
        in_specs=[pl.BlockSpec((tm, d), lambda i: (i, 0)),
                  pl.BlockSpec((1, d), lambda i: (0, 0))],
        out_specs=pl.BlockSpec((tm, d), lambda i: (i, 0)),
        out_shape=jax.ShapeDtypeStruct((t, d), BF16),
        compiler_params=_params("parallel"),
        name="rmsnorm",
    )(x, gain.reshape(1, d))


def _rmsnorm_dilated_kernel(x_ref, g_ref, o1_ref, o4_ref, o16_ref, scr):
    x = x_ref[...]
    ms = jnp.mean(x * x, axis=-1, keepdims=True)
    y = x * lax.rsqrt(ms + RMS_EPS) * g_ref[...]
    o1_ref[...] = y.astype(o1_ref.dtype)
    n_slabs = scr.shape[0]
    for s in range(n_slabs):
        scr[s] = y[:, s * LANES:(s + 1) * LANES]
    for o_ref in (o4_ref, o16_ref):
        dil, rows = o_ref.shape[0], o_ref.shape[1]
        for r in range(dil):
            for s in range(n_slabs):
                o_ref[r, :, s * LANES:(s + 1) * LANES] = (
                    scr[s, pl.ds(r, rows, stride=dil), :].astype(o_ref.dtype))


def rmsnorm_dilated(x, gain, bsz, seq, tm=256):
    t, d = x.shape
    tiles = seq // tm
    d4, d16 = ATTN_PATTERNS[1][1], ATTN_PATTERNS[2][1]
    o1, o4, o16 = pl.pallas_call(
        _rmsnorm_dilated_kernel,
        grid=(bsz, tiles),
        in_specs=[pl.BlockSpec((tm, d), lambda b, i: (b * tiles + i, 0)),
                  pl.BlockSpec((1, d), lambda b, i: (0, 0))],
        out_specs=[pl.BlockSpec((tm, d), lambda b, i: (b * tiles + i, 0)),
                   pl.BlockSpec((None, d4, tm // d4, d), lambda b, i: (b, 0, i, 0)),
                   pl.BlockSpec((None, d16, tm // d16, d), lambda b, i: (b, 0, i, 0))],
        out_shape=[jax.ShapeDtypeStruct((t, d), BF16),
                   jax.ShapeDtypeStruct((bsz, d4, seq // d4, d), BF16),
                   jax.ShapeDtypeStruct((bsz, d16, seq // d16, d), BF16)],
        scratch_shapes=[pltpu.VMEM((d // LANES, tm, LANES), F32)],
        compiler_params=_params("parallel", "parallel"),
        name="rmsnorm_dilated",
    )(x, gain.reshape(1, d))
    return o1, o4.reshape(t, d), o16.reshape(t, d)


def _mm_kernel(a_ref, w_ref, o_ref):
    o_ref[...] = jnp.dot(a_ref[...], w_ref[...], preferred_element_type=F32).astype(o_ref.dtype)


def matmul(a, w, out_dtype, col0=0, n=None, tm=1024, tn=512):
    m, k = a.shape
    n = w.shape[1] if n is None else n
    j0 = col0 // tn
    assert j0 * tn == col0
    return pl.pallas_call(
        _mm_kernel,
        grid=(m // tm, n // tn),
        in_specs=[pl.BlockSpec((tm, k), lambda i, j: (i, 0)),
                  pl.BlockSpec((k, tn), lambda i, j: (0, j0 + j))],
        out_specs=pl.BlockSpec((tm, tn), lambda i, j: (i, j)),
        out_shape=jax.ShapeDtypeStruct((m, n), out_dtype),
        compiler_params=_params("parallel", "parallel"),
        name="matmul",
    )(a, w)


def _mm_res_kernel(a_ref, w_ref, r_ref, o_ref):
    o_ref[...] = r_ref[...] + jnp.dot(a_ref[...], w_ref[...], preferred_element_type=F32)


def matmul_residual(a, w, res, tm=1024, tn=512):
    m, k = a.shape
    n = w.shape[1]
    return pl.pallas_call(
        _mm_res_kernel,
        grid=(m // tm, n // tn),
        in_specs=[pl.BlockSpec((tm, k), lambda i, j: (i, 0)),
                  pl.BlockSpec((k, tn), lambda i, j: (0, j)),
                  pl.BlockSpec((tm, tn), lambda i, j: (i, j))],
        out_specs=pl.BlockSpec((tm, tn), lambda i, j: (i, j)),
        out_shape=jax.ShapeDtypeStruct((m, n), F32),
        compiler_params=_params("parallel", "parallel"),
        name="matmul_residual",
    )(a, w, res)


def _dt_kernel(a_ref, w_ref, b_ref, o_ref):
    raw = jnp.dot(a_ref[...], w_ref[...], preferred_element_type=F32) + b_ref[...]
    o_ref[...] = jnp.maximum(raw, 0.0) + jnp.log(1.0 + jnp.exp(-jnp.abs(raw)))


def dt_projection(a, w, bias, tm=1024):
    m, k = a.shape
    n = w.shape[1]
    return pl.pallas_call(
        _dt_kernel,
        grid=(m // tm,),
        in_specs=[pl.BlockSpec((tm, k), lambda i: (i, 0)),
                  pl.BlockSpec((k, n), lambda i: (0, 0)),
                  pl.BlockSpec((1, n), lambda i: (0, 0))],
        out_specs=pl.BlockSpec((tm, n), lambda i: (i, 0)),
        out_shape=jax.ShapeDtypeStruct((m, n), F32),
        compiler_params=_params("parallel"),
        name="dt_projection",
    )(a, w, bias)


def _conv_proj_kernel(a_ref, w_ref, cw_ref, cb_ref, o_ref, acc0, acc1, tail_scr, *, tiles_per_seq, rows=64):
    i = pl.program_id(1)
    tm = a_ref.shape[0]
    halo = tail_scr.shape[0]

    @pl.when(lax.rem(i, tiles_per_seq) == 0)
    def _():
        tail_scr[...] = jnp.zeros(tail_scr.shape, F32)

    def project(dst, cs):
        dst[...] = jnp.dot(a_ref[...], w_ref[:, cs], preferred_element_type=F32)

    def conv_silu(src, cs):
        cw = cw_ref[:, cs]
        bias = cb_ref[:, cs]
        zero = pl.multiple_of(jnp.minimum(i, 0) * 8, 8)
        for c in range(tm // rows):
            r0 = c * rows
            head = tail_scr[:, cs] if c == 0 else src[pl.ds(r0 - halo + zero, halo), :]
            u = src[pl.ds(r0 + zero, rows), :]
            ext = jnp.concatenate([head, u], axis=0)
            y = bias + cw[SSM_CONV_WIDTH - 1:SSM_CONV_WIDTH, :] * u
            for k in range(SSM_CONV_WIDTH - 1):
                shift = SSM_CONV_WIDTH - 1 - k
                y = y + cw[k:k + 1, :] * pltpu.roll(ext, shift, axis=0)[halo:, :]
            o_ref[r0:r0 + rows, cs] = (y * _sigmoid(y)).astype(o_ref.dtype)
        tail_scr[:, cs] = src[tm - halo:, :]

    accs = (acc0, acc1)
    width = acc0.shape[1]
    n_chunks = o_ref.shape[1] // width
    for nc in range(n_chunks):
        cs = slice(nc * width, (nc + 1) * width)
        project(accs[nc % 2], cs)
        if nc > 0:
            conv_silu(accs[(nc - 1) % 2], slice((nc - 1) * width, nc * width))
    conv_silu(accs[(n_chunks - 1) % 2], slice((n_chunks - 1) * width, n_chunks * width))


def conv_projection(a, w, col0, n, conv_w, conv_b, seq, tm=1024, tn=2048, chunk=256):
    m, k = a.shape
    j0 = col0 // tn
    assert j0 * tn == col0
    return pl.pallas_call(
        functools.partial(_conv_proj_kernel, tiles_per_seq=seq // tm),
        grid=(n // tn, m // tm),
        in_specs=[pl.BlockSpec((tm, k), lambda j, i: (i, 0)),
                  pl.BlockSpec((k, tn), lambda j, i: (0, j0 + j)),
                  pl.BlockSpec((SSM_CONV_WIDTH, tn), lambda j, i: (0, j)),
                  pl.BlockSpec((1, tn), lambda j, i: (0, j))],
        out_specs=pl.BlockSpec((tm, tn), lambda j, i: (i, j)),
        out_shape=jax.ShapeDtypeStruct((m, n), BF16),
        scratch_shapes=[pltpu.VMEM((tm, chunk), F32), pltpu.VMEM((tm, chunk), F32),
                        pltpu.VMEM((8, tn), F32)],
        compiler_params=_params("parallel", "arbitrary"),
        name="conv_projection",
    )(a, w, conv_w, conv_b.reshape(1, n))


def _ssd_kernel(xbc_ref, z_ref, dt_ref, a_ref, dskip_ref, nw_ref, expand_ref, o_ref, state_ref,
                *, d_inner):
    c = pl.program_id(1)
    n_groups = SSM_N_GROUPS
    gw = d_inner // n_groups
    pair = 2 * SSM_HEAD_DIM
    lq = SSM_CHUNK

    @pl.when(c == 0)
    def _():
        state_ref[...] = jnp.zeros(state_ref.shape, F32)

    dt = dt_ref[...]
    a = dt * a_ref[...]
    row_i = lax.broadcasted_iota(jnp.int32, (lq, lq), 0)
    col_i = lax.broadcasted_iota(jnp.int32, (lq, lq), 1)
    causal = row_i >= col_i
    tril = jnp.where(causal, 1.0, 0.0).astype(BF16)
    tril3 = jnp.concatenate([tril, tril, tril], axis=1)
    acum = jnp.dot(tril3, jnp.concatenate(_split3(a), axis=0),
                   preferred_element_type=F32)
    acum_t = acum.T
    dt_t = dt.T
    alast = acum[lq - 1:lq, :]
    stacked = jnp.concatenate([
        jnp.exp(acum),
        dt * jnp.exp(alast - acum),
        jnp.broadcast_to(jnp.exp(alast), (8, LANES)),
    ], axis=0)
    wide = jnp.dot(jnp.concatenate(_split3(stacked), axis=1), expand_ref[...],
                   preferred_element_type=F32)
    decay_in_w = wide[0:lq, :]
    dt_decay_out_w = wide[lq:2 * lq, :]
    chunk_decay_w = wide[2 * lq:2 * lq + 1, :]
    lane = lax.broadcasted_iota(jnp.int32, (lq, pair), 1)
    lo_half = lane < SSM_HEAD_DIM

    for g in range(n_groups):
        sl = slice(g * gw, (g + 1) * gw)
        xs = xbc_ref[:, sl]
        bm = xbc_ref[:, d_inner + g * SSM_D_STATE:d_inner + (g + 1) * SSM_D_STATE]
        cm = xbc_ref[:, d_inner + (n_groups + g) * SSM_D_STATE:
                     d_inner + (n_groups + g + 1) * SSM_D_STATE]
        xs32 = xs.astype(F32)
        cb = lax.dot_general(cm, bm, (((1,), (1,)), ((), ())), preferred_element_type=F32)
        state = state_ref[g]
        y = jnp.dot(cm, state.astype(BF16), preferred_element_type=F32) * decay_in_w[:, sl]
        xw = xs * dt_decay_out_w[:, sl].astype(BF16)
        bm_t = bm.astype(F32).T.astype(BF16)
        state_ref[g] = state * chunk_decay_w[:, sl] + jnp.dot(bm_t, xw, preferred_element_type=F32)

        y_pairs = []
        for hp in range(gw // pair):
            xs_pair = xs[:, hp * pair:(hp + 1) * pair]
            zero = jnp.zeros_like(xs_pair)
            mats = []
            for half in range(2):
                h = (g * gw + hp * pair) // SSM_HEAD_DIM + half
                seg = acum[:, h:h + 1] - acum_t[h:h + 1, :]
                decay = jnp.exp(jnp.where(causal, seg, -jnp.inf))
                mats.append((decay * cb * dt_t[h:h + 1, :]).astype(BF16))
            rhs = jnp.concatenate([jnp.where(lo_half, xs_pair, zero),
                                   jnp.where(lo_half, zero, xs_pair)], axis=0)
            y_pairs.append(jnp.dot(jnp.concatenate(mats, axis=1), rhs, preferred_element_type=F32))
        y = y + jnp.concatenate(y_pairs, axis=1)
        y = y + xs32 * dskip_ref[:, sl]
        zg = z_ref[:, sl].astype(F32)
        y = y * (zg * _sigmoid(zg))
        ms = jnp.mean(y * y, axis=-1, keepdims=True)
        o_ref[:, sl] = (y * lax.rsqrt(ms + RMS_EPS) * nw_ref[:, sl]).astype(o_ref.dtype)


def ssd_mixer_core(xbc, z, dt, a_row, dskip_w, norm_w, expand, bsz, seq):
    t, d_inner = z.shape
    n_chunks = seq // SSM_CHUNK
    gw = d_inner // SSM_N_GROUPS
    row = lambda b, c: (b * n_chunks + c, 0)
    fixed = lambda b, c: (0, 0)
    return pl.pallas_call(
        functools.partial(_ssd_kernel, d_inner=d_inner),
        grid=(bsz, n_chunks),
        in_specs=[pl.BlockSpec((SSM_CHUNK, xbc.shape[1]), row),
                  pl.BlockSpec((SSM_CHUNK, d_inner), row),
                  pl.BlockSpec((SSM_CHUNK, LANES), row),
                  pl.BlockSpec((1, LANES), fixed),
                  pl.BlockSpec((1, d_inner), fixed),
                  pl.BlockSpec((1, d_inner), fixed),
                  pl.BlockSpec((3 * LANES, d_inner), fixed)],
        out_specs=pl.BlockSpec((SSM_CHUNK, d_inner), row),
        out_shape=jax.ShapeDtypeStruct((t, d_inner), BF16),
        scratch_shapes=[pltpu.VMEM((SSM_N_GROUPS, SSM_D_STATE, gw), F32)],
        compiler_params=_params("parallel", "arbitrary"),
        name="ssd_mixer_core",
    )(xbc, z, dt, a_row, dskip_w, norm_w, expand)


def _ffn_kernel(h_ref, x_ref, wg_ref, wu_ref, wd_ref, o_ref):
    @pl.when(pl.program_id(1) == 0)
    def _():
        o_ref[...] = x_ref[...]

    h = h_ref[...]
    gate = jnp.dot(h, wg_ref[...], preferred_element_type=F32)
    up = jnp.dot(h, wu_ref[...], preferred_element_type=F32)
    act = (gate * _sigmoid(gate) * up).astype(BF16)
    o_ref[...] += jnp.dot(act, wd_ref[...], preferred_element_type=F32)


def swiglu_ffn_residual(h, x, w_gate, w_up, w_down, tm=512, tf=512):
    m, d = h.shape
    d_ff = w_gate.shape[1]
    return pl.pallas_call(
        _ffn_kernel,
        grid=(m // tm, d_ff // tf),
        in_specs=[pl.BlockSpec((tm, d), lambda i, f: (i, 0)),
                  pl.BlockSpec((tm, d), lambda i, f: (i, 0)),
                  pl.BlockSpec((d, tf), lambda i, f: (0, f)),
                  pl.BlockSpec((d, tf), lambda i, f: (0, f)),
                  pl.BlockSpec((tf, d), lambda i, f: (f, 0))],
        out_specs=pl.BlockSpec((tm, d), lambda i, f: (i, 0)),
        out_shape=jax.ShapeDtypeStruct((m, d), F32),
        compiler_params=_params("parallel", "arbitrary"),
        name="swiglu_ffn",
    )(h, x, w_gate, w_up, w_down)


def _qk_kernel(a_ref, w_ref, gain_ref, cos_ref, sin_ref, o_ref, *, rows):
    w = w_ref[...]
    gain = gain_ref[...]
    hd = ATTN_HEAD_DIM
    for c in range(a_ref.shape[0] // rows):
        rs = slice(c * rows, (c + 1) * rows)
        acc = jnp.dot(a_ref[rs, :], w, preferred_element_type=F32)
        cos = cos_ref[rs, :]
        sin = sin_ref[rs, :]
        for h in range(acc.shape[1] // hd):
            t = acc[:, h * hd:(h + 1) * hd]
            ms = jnp.mean(t * t, axis=-1, keepdims=True)
            tn = t * lax.rsqrt(ms + RMS_EPS) * gain
            rot = tn * cos + pltpu.roll(tn, hd // 2, axis=1) * sin
            o_ref[rs, h * hd:(h + 1) * hd] = rot.astype(o_ref.dtype)


def qk_projection(a, w, col0, n, gains, cos, sin, seq, tm=1024, rows=256):
    m, k = a.shape
    tiles = seq // tm
    j0 = col0 // n
    assert j0 * n == col0
    return pl.pallas_call(
        functools.partial(_qk_kernel, rows=rows),
        grid=(m // tm, 2),
        in_specs=[pl.BlockSpec((tm, k), lambda i, j: (i, 0)),
                  pl.BlockSpec((k, n), lambda i, j: (0, j0 + j)),
                  pl.BlockSpec((None, 1, ATTN_HEAD_DIM), lambda i, j: (j, 0, 0)),
                  pl.BlockSpec((tm, ATTN_HEAD_DIM), lambda i, j: (i % tiles, 0)),
                  pl.BlockSpec((tm, ATTN_HEAD_DIM), lambda i, j: (i % tiles, 0))],
        out_specs=pl.BlockSpec((None, tm, n), lambda i, j: (j, i, 0)),
        out_shape=jax.ShapeDtypeStruct((2, m, n), BF16),
        compiler_params=_params("parallel", "arbitrary"),
        name="qk_projection",
    )(a, w, gains, cos, sin)


def _attn_kernel(*refs, seq):
    n_groups = len(ATTN_PATTERNS)
    qkv = refs[:3 * n_groups]
    o_ref = refs[3 * n_groups]
    out_scr, lse_scr, vext_all = refs[3 * n_groups + 1:]
    blk = ATTN_BLOCK
    hd = ATTN_HEAD_DIM
    scale = hd ** -0.5
    qi = lax.broadcasted_iota(jnp.int32, (blk, blk), 0)
    kj = lax.broadcasted_iota(jnp.int32, (blk, blk), 1)
    own_ok = kj <= qi
    prev_ok = kj >= qi
    contract_last = (((1,), (1,)), ((), ()))
    for g in range(n_groups):
        vext_all[g, :, hd:] = jnp.ones((seq, hd), BF16)

    def attend(q_ref, k_ref, vext_scr, blocks):
        qs = [q_ref[p0:p0 + blk, :] for p0, _ in blocks]
        s_own = [lax.dot_general(q, k_ref[p0:p0 + blk, :], contract_last, preferred_element_type=F32)
                 for q, (p0, _) in zip(qs, blocks)]
        s_prev = [lax.dot_general(q, k_ref[p0 - blk:p0, :], contract_last, preferred_element_type=F32)
                  if has_prev else None for q, (p0, has_prev) in zip(qs, blocks)]
        results = []
        for so, sp, (p0, has_prev) in zip(s_own, s_prev, blocks):
            so = jnp.where(own_ok, so * scale, -jnp.inf)
            if has_prev:
                sp = jnp.where(prev_ok, sp * scale, -jnp.inf)
                m = jnp.max(jnp.maximum(so, sp), axis=-1, keepdims=True)
            else:
                m = jnp.max(so, axis=-1, keepdims=True)
            ext = jnp.dot(jnp.exp(so - m).astype(BF16), vext_scr[p0:p0 + blk, :],
                          preferred_element_type=F32)
            if has_prev:
                ext = ext + jnp.dot(jnp.exp(sp - m).astype(BF16), vext_scr[p0 - blk:p0, :],
                                    preferred_element_type=F32)
            denom = ext[:, hd:]
            results.append((ext[:, :hd] / denom, m + jnp.log(denom)))
        return results

    batch = 4
    for g, (window, dil) in enumerate(ATTN_PATTERNS):
        assert window // dil == blk
        q_ref, k_ref, v_ref = qkv[3 * g:3 * g + 3]
        vext_scr = vext_all.at[g]
        vext_scr[:, :hd] = v_ref[...]
        nb = seq // dil // blk
        blocks = [((r * nb + i) * blk, i > 0, i * blk * dil + r) for r in range(dil) for i in range(nb)]
        for b0 in range(0, len(blocks), batch):
            chunk = blocks[b0:b0 + batch]
            for (out, lse), (p0, _, t0) in zip(attend(q_ref, k_ref, vext_scr, [c[:2] for c in chunk]), chunk):
                dst = pl.ds(p0, blk) if dil == 1 else pl.ds(t0, blk, stride=dil)
                out_scr[g, dst, :] = out
                lse_scr[g, dst, :] = lse

    lse_all = [lse_scr[g] for g in range(n_groups)]
    top = functools.reduce(jnp.maximum, lse_all)
    num = jnp.zeros((seq, hd), F32)
    den = jnp.zeros((seq, hd), F32)
    for g in range(n_groups):
        wgt = jnp.exp(lse_all[g] - top)
        num = num + wgt * out_scr[g]
        den = den + wgt
    o_ref[...] = (num / den).astype(o_ref.dtype)


def dilated_attention(qk_groups, v_groups, bsz, seq):
    t, n = v_groups[0].shape
    heads = n // ATTN_HEAD_DIM
    in_specs, args = [], []
    for qk, v in zip(qk_groups, v_groups):
        for which in range(2):
            in_specs.append(pl.BlockSpec((None, seq, ATTN_HEAD_DIM),
                                         lambda b, h, which=which: (which, b, h)))
            args.append(qk)
        in_specs.append(pl.BlockSpec((seq, ATTN_HEAD_DIM), lambda b, h: (b, h)))
        args.append(v)
    n_groups = len(v_groups)
    return pl.pallas_call(
        functools.partial(_attn_kernel, seq=seq),
        grid=(bsz, heads),
        in_specs=in_specs,
        out_specs=pl.BlockSpec((seq, ATTN_HEAD_DIM), lambda b, h: (b, h)),
        out_shape=jax.ShapeDtypeStruct((t, n), BF16),
        scratch_shapes=[pltpu.VMEM((n_groups, seq, ATTN_HEAD_DIM), F32),
                        pltpu.VMEM((n_groups, seq, ATTN_HEAD_DIM), F32),
                        pltpu.VMEM((n_groups, seq, 2 * ATTN_HEAD_DIM), BF16)],
        compiler_params=_params("parallel", "parallel"),
        name="dilated_attention",
    )(*args)


def _dilate_rows(table, dil):
    seq, c = table.shape
    return table.reshape(seq // dil, dil, c).transpose(1, 0, 2).reshape(seq, c)


def _rope_tables(seq):
    half = ATTN_HEAD_DIM // 2
    inv_freq = ROPE_THETA ** (-jnp.arange(half, dtype=F32) / half)
    ang = jnp.arange(seq, dtype=F32)[:, None] * inv_freq[None, :]
    cos, sin = jnp.cos(ang), jnp.sin(ang)
    return jnp.concatenate([cos, cos], axis=1), jnp.concatenate([-sin, sin], axis=1)


def mamba2_layer(x, h, bsz, seq, w_in, conv_w, conv_b, dt_bias, a_log, d_skip, norm_w, w_out):
    d_inner = w_out.shape[0]
    n_heads = d_inner // SSM_HEAD_DIM
    conv_dim = conv_w.shape[1]
    d_model = w_in.shape[0]
    w_in_bf = w_in.astype(BF16)
    w_dt = jnp.zeros((d_model, LANES), BF16).at[:, :n_heads].set(w_in_bf[:, d_inner + conv_dim:])
    dt_bias_row = jnp.zeros((1, LANES), F32).at[0, :n_heads].set(dt_bias.astype(F32))
    a_row = jnp.zeros((1, LANES), F32).at[0, :n_heads].set(-jnp.exp(a_log.astype(F32)))
    dskip_w = jnp.repeat(d_skip.astype(F32), SSM_HEAD_DIM).reshape(1, d_inner)
    head_of_channel = jnp.arange(d_inner) // SSM_HEAD_DIM
    expand = (jnp.arange(LANES)[:, None] == head_of_channel[None, :]).astype(BF16)
    expand = jnp.concatenate([expand, expand, expand], axis=0)

    z = matmul(h, w_in_bf, BF16, col0=0, n=d_inner)
    xbc = conv_projection(h, w_in_bf, d_inner, conv_dim, conv_w.astype(F32), conv_b.astype(F32), seq)
    dt = dt_projection(h, w_dt, dt_bias_row)
    y = ssd_mixer_core(xbc, z, dt, a_row, dskip_w, norm_w.astype(F32).reshape(1, d_inner), expand,
                       bsz, seq)
    return matmul_residual(y, w_out.astype(BF16), x)


def attention_layer(x, hs, bsz, seq, w_qkv, q_norm, k_norm, w_out):
    d_model = w_qkv.shape[0]
    n_groups = len(ATTN_PATTERNS)
    n = ATTN_GROUP_HEADS * ATTN_HEAD_DIM
    w = w_qkv.astype(BF16)
    cos, sin = _rope_tables(seq)
    qk_groups, v_groups = [], []
    for g, (_, dil) in enumerate(ATTN_PATTERNS):
        gains = jnp.stack([q_norm[g], k_norm[g]]).astype(F32).reshape(2, 1, ATTN_HEAD_DIM)
        col0 = g * 3 * n
        qk_groups.append(qk_projection(hs[g], w, col0, n, gains, _dilate_rows(cos, dil),
                                       _dilate_rows(sin, dil), seq))
        v_groups.append(matmul(hs[g], w, BF16, col0=col0 + 2 * n, n=n))
    merged = dilated_attention(qk_groups, v_groups, bsz, seq)
    return matmul_residual(merged, w_out.astype(BF16), x)


def kernel(x, mix_norm, ffn_norm, ssm_w_in, ssm_conv_w, ssm_conv_b, ssm_dt_bias, ssm_a_log, ssm_d, ssm_norm, ssm_w_out, attn_w_qkv, attn_q_norm, attn_k_norm, attn_w_out, ffn_w_gate, ffn_w_up, ffn_w_down):
    bsz, seq, d_model = x.shape
    depth = mix_norm.shape[0]
    xt = x.reshape(bsz * seq, d_model).astype(F32)
    for layer in range(depth):
        j = layer // 2
        if layer % 2 == 0:
            h = rmsnorm(xt, mix_norm[layer].astype(F32))
            xt = mamba2_layer(xt, h, bsz, seq, ssm_w_in[j], ssm_conv_w[j], ssm_conv_b[j], ssm_dt_bias[j],
                              ssm_a_log[j], ssm_d[j], ssm_norm[j], ssm_w_out[j])
        else:
            hs = rmsnorm_dilated(xt, mix_norm[layer].astype(F32), bsz, seq)
            xt = attention_layer(xt, hs, bsz, seq, attn_w_qkv[j], attn_q_norm[j], attn_k_norm[j],
                                 attn_w_out[j])
        h = rmsnorm(xt, ffn_norm[layer].astype(F32))
        xt = swiglu_ffn_residual(h, xt, ffn_w_gate[layer].astype(BF16), ffn_w_up[layer].astype(BF16),
                                 ffn_w_down[layer].astype(BF16))
    return xt.reshape(bsz, seq, d_model).astype(x.dtype)
```

```python
import functools
import math

import jax
import jax.numpy as jnp
from jax import lax
from jax.experimental import pallas as pl
from jax.experimental.pallas import tpu as pltpu

F32 = jnp.float32
BF16 = jnp.bfloat16

RMS_EPS = 1e-6
LANES = 128
VMEM_LIMIT_BYTES = 56 * 1024 * 1024

SSM_HEAD_DIM = 64
SSM_D_STATE = 128
SSM_N_GROUPS = 8
SSM_CONV_WIDTH = 4
SSM_CHUNK = 128

ATTN_HEAD_DIM = 128
ATTN_GROUP_HEADS = 8
ATTN_PATTERNS = ((128, 1), (512, 4), (2048, 16))
ATTN_BLOCK = 128
ROPE_THETA = 10000.0


def _params(*sem):
    return pltpu.CompilerParams(dimension_semantics=sem, vmem_limit_bytes=VMEM_LIMIT_BYTES)


def _sigmoid(v):
    return 1.0 / (1.0 + jnp.exp(-v))


def _split3(v):
    hi = v.astype(BF16)
    r1 = v - hi.astype(F32)
    mid = r1.astype(BF16)
    lo = (r1 - mid.astype(F32)).astype(BF16)
    return hi, mid, lo


def _rmsnorm_kernel(x_ref, g_ref, o_ref):
    x = x_ref[...]
    ms = jnp.mean(x * x, axis=-1, keepdims=True)
    o_ref[...] = (x * lax.rsqrt(ms + RMS_EPS) * g_ref[...]).astype(o_ref.dtype)


def rmsnorm(x, gain, tm=512):
    t, d = x.shape
    return pl.pallas_call(
        _rmsnorm_kernel,
        grid=(t // tm,),
        in_specs=[pl.BlockSpec((tm, d), lambda i: (i, 0)),
                  pl.BlockSpec((1, d), lambda i: (0, 0))],
        out_specs=pl.BlockSpec((tm, d), lambda i: (i, 0)),
        out_shape=jax.ShapeDtypeStruct((t, d), BF16),
        compiler_params=_params("parallel"),
        name="rmsnorm",
    )(x, gain.reshape(1, d))


def _rmsnorm_dilated_kernel(x_ref, g_ref, o1_ref, o4_ref, o16_ref, scr):
    x = x_ref[...]
    ms = jnp.mean(x * x, axis=-1, keepdims=True)
    y = x * lax.rsqrt(ms + RMS_EPS) * g_ref[...]
    o1_ref[...] = y.astype(o1_ref.dtype)
    n_slabs = scr.shape[0]
    for s in range(n_slabs):
        scr[s] = y[:, s * LANES:(s + 1) * LANES]
    for o_ref in (o4_ref, o16_ref):
        dil, rows = o_ref.shape[0], o_ref.shape[1]
        for r in range(dil):
            for s in range(n_slabs):
                o_ref[r, :, s * LANES:(s + 1) * LANES] = (
                    scr[s, pl.ds(r, rows, stride=dil), :].astype(o_ref.dtype))


def rmsnorm_dilated(x, gain, bsz, seq, tm=256):
    t, d = x.shape
    tiles = seq // tm
    d4, d16 = ATTN_PATTERNS[1][1], ATTN_PATTERNS[2][1]
    o1, o4, o16 = pl.pallas_call(
        _rmsnorm_dilated_kernel,
        grid=(bsz, tiles),
        in_specs=[pl.BlockSpec((tm, d), lambda b, i: (b * tiles + i, 0)),
                  pl.BlockSpec((1, d), lambda b, i: (0, 0))],
        out_specs=[pl.BlockSpec((tm, d), lambda b, i: (b * tiles + i, 0)),
                   pl.BlockSpec((None, d4, tm // d4, d), lambda b, i: (b, 0, i, 0)),
                   pl.BlockSpec((None, d16, tm // d16, d), lambda b, i: (b, 0, i, 0))],
        out_shape=[jax.ShapeDtypeStruct((t, d), BF16),
                   jax.ShapeDtypeStruct((bsz, d4, seq // d4, d), BF16),
                   jax.ShapeDtypeStruct((bsz, d16, seq // d16, d), BF16)],
        scratch_shapes=[pltpu.VMEM((d // LANES, tm, LANES), F32)],
        compiler_params=_params("parallel", "parallel"),
        name="rmsnorm_dilated",
    )(x, gain.reshape(1, d))
    return o1, o4.reshape(t, d), o16.reshape(t, d)


def _mm_kernel(a_ref, w_ref, o_ref):
    o_ref[...] = jnp.dot(a_ref[...], w_ref[...], preferred_element_type=F32).astype(o_ref.dtype)


def matmul(a, w, out_dtype, col0=0, n=None, tm=1024, tn=1024):
    m, k = a.shape
    n = w.shape[1] if n is None else n
    j0 = col0 // tn
    assert j0 * tn == col0
    return pl.pallas_call(
        _mm_kernel,
        grid=(m // tm, n // tn),
        in_specs=[pl.BlockSpec((tm, k), lambda i, j: (i, 0)),
                  pl.BlockSpec((k, tn), lambda i, j: (0, j0 + j))],
        out_specs=pl.BlockSpec((tm, tn), lambda i, j: (i, j)),
        out_shape=jax.ShapeDtypeStruct((m, n), out_dtype),
        compiler_params=_params("parallel", "parallel"),
        name="matmul",
    )(a, w)


def _mm_res_norm_kernel(a_ref, w_ref, r_ref, g_ref, o_ref, h_ref, *, n_k):
    k = pl.program_id(1)

    @pl.when(k == 0)
    def _():
        o_ref[...] = r_ref[...]

    o_ref[...] += jnp.dot(a_ref[...], w_ref[...], preferred_element_type=F32)

    @pl.when(k == n_k - 1)
    def _():
        x = o_ref[...]
        ms = jnp.mean(x * x, axis=-1, keepdims=True)
        h_ref[...] = (x * lax.rsqrt(ms + RMS_EPS) * g_ref[...]).astype(h_ref.dtype)


def matmul_residual_norm(a, w, res, gain, tm=256, tk=None):
    m, k = a.shape
    n = w.shape[1]
    tk = k if tk is None else tk
    n_k = k // tk
    return pl.pallas_call(
        functools.partial(_mm_res_norm_kernel, n_k=n_k),
        grid=(m // tm, n_k),
        in_specs=[pl.BlockSpec((tm, tk), lambda i, kk: (i, kk)),
                  pl.BlockSpec((tk, n), lambda i, kk: (kk, 0)),
                  pl.BlockSpec((tm, n), lambda i, kk: (i, 0)),
                  pl.BlockSpec((1, n), lambda i, kk: (0, 0))],
        out_specs=[pl.BlockSpec((tm, n), lambda i, kk: (i, 0)),
                   pl.BlockSpec((tm, n), lambda i, kk: (i, 0))],
        out_shape=[jax.ShapeDtypeStruct((m, n), F32), jax.ShapeDtypeStruct((m, n), BF16)],
        compiler_params=_params("parallel", "arbitrary"),
        name="matmul_residual_norm",
    )(a, w, res, gain.reshape(1, n))


def _dt_kernel(a_ref, w_ref, b_ref, o_ref):
    raw = jnp.dot(a_ref[...], w_ref[...], preferred_element_type=F32) + b_ref[...]
    o_ref[...] = jnp.maximum(raw, 0.0) + jnp.log(1.0 + jnp.exp(-jnp.abs(raw)))


def dt_projection(a, w, bias, tm=1024):
    m, k = a.shape
    n = w.shape[1]
    return pl.pallas_call(
        _dt_kernel,
        grid=(m // tm,),
        in_specs=[pl.BlockSpec((tm, k), lambda i: (i, 0)),
                  pl.BlockSpec((k, n), lambda i: (0, 0)),
                  pl.BlockSpec((1, n), lambda i: (0, 0))],
        out_specs=pl.BlockSpec((tm, n), lambda i: (i, 0)),
        out_shape=jax.ShapeDtypeStruct((m, n), F32),
        compiler_params=_params("parallel"),
        name="dt_projection",
    )(a, w, bias)


def _conv_proj_kernel(a_ref, w_ref, cw_ref, cb_ref, o_ref, acc0, acc1, tail_scr, *, tiles_per_seq, rows=64):
    i = pl.program_id(1)
    tm = a_ref.shape[0]
    halo = tail_scr.shape[0]

    @pl.when(lax.rem(i, tiles_per_seq) == 0)
    def _():
        tail_scr[...] = jnp.zeros(tail_scr.shape, F32)

    def project(dst, cs):
        dst[...] = jnp.dot(a_ref[...], w_ref[:, cs], preferred_element_type=F32)

    def conv_silu(src, cs):
        cw = cw_ref[:, cs]
        bias = cb_ref[:, cs]
        zero = pl.multiple_of(jnp.minimum(i, 0) * 8, 8)
        for c in range(tm // rows):
            r0 = c * rows
            head = tail_scr[:, cs] if c == 0 else src[pl.ds(r0 - halo + zero, halo), :]
            u = src[pl.ds(r0 + zero, rows), :]
            ext = jnp.concatenate([head, u], axis=0)
            y = bias + cw[SSM_CONV_WIDTH - 1:SSM_CONV_WIDTH, :] * u
            for k in range(SSM_CONV_WIDTH - 1):
                shift = SSM_CONV_WIDTH - 1 - k
                y = y + cw[k:k + 1, :] * pltpu.roll(ext, shift, axis=0)[halo:, :]
            o_ref[r0:r0 + rows, cs] = (y * _sigmoid(y)).astype(o_ref.dtype)
        tail_scr[:, cs] = src[tm - halo:, :]

    accs = (acc0, acc1)
    width = acc0.shape[1]
    n_chunks = o_ref.shape[1] // width
    for nc in range(n_chunks):
        cs = slice(nc * width, (nc + 1) * width)
        project(accs[nc % 2], cs)
        if nc > 0:
            conv_silu(accs[(nc - 1) % 2], slice((nc - 1) * width, nc * width))
    conv_silu(accs[(n_chunks - 1) % 2], slice((n_chunks - 1) * width, n_chunks * width))


def conv_projection(a, w, col0, n, conv_w, conv_b, seq, tm=1024, tn=2048, chunk=256):
    m, k = a.shape
    j0 = col0 // tn
    assert j0 * tn == col0
    return pl.pallas_call(
        functools.partial(_conv_proj_kernel, tiles_per_seq=seq // tm),
        grid=(n // tn, m // tm),
        in_specs=[pl.BlockSpec((tm, k), lambda j, i: (i, 0)),
                  pl.BlockSpec((k, tn), lambda j, i: (0, j0 + j)),
                  pl.BlockSpec((SSM_CONV_WIDTH, tn), lambda j, i: (0, j)),
                  pl.BlockSpec((1, tn), lambda j, i: (0, j))],
        out_specs=pl.BlockSpec((tm, tn), lambda j, i: (i, j)),
        out_shape=jax.ShapeDtypeStruct((m, n), BF16),
        scratch_shapes=[pltpu.VMEM((tm, chunk), F32), pltpu.VMEM((tm, chunk), F32),
                        pltpu.VMEM((8, tn), F32)],
        compiler_params=_params("parallel", "arbitrary"),
        name="conv_projection",
    )(a, w, conv_w, conv_b.reshape(1, n))


def _ssd_kernel(xbc_ref, z_ref, dt_ref, a_ref, dskip_ref, nw_ref, expand_ref, o_ref, state_ref,
                *, d_inner):
    c = pl.program_id(1)
    n_groups = SSM_N_GROUPS
    gw = d_inner // n_groups
    pair = 2 * SSM_HEAD_DIM
    lq = SSM_CHUNK

    @pl.when(c == 0)
    def _():
        state_ref[...] = jnp.zeros(state_ref.shape, F32)

    dt = dt_ref[...]
    a = dt * a_ref[...]
    row_i = lax.broadcasted_iota(jnp.int32, (lq, lq), 0)
    col_i = lax.broadcasted_iota(jnp.int32, (lq, lq), 1)
    causal = row_i >= col_i
    tril = jnp.where(causal, 1.0, 0.0).astype(BF16)
    tril3 = jnp.concatenate([tril, tril, tril], axis=1)
    acum = jnp.dot(tril3, jnp.concatenate(_split3(a), axis=0),
                   preferred_element_type=F32)
    acum_t = acum.T
    dt_t = dt.T
    alast = acum[lq - 1:lq, :]
    stacked = jnp.concatenate([
        jnp.exp(acum),
        dt * jnp.exp(alast - acum),
        jnp.broadcast_to(jnp.exp(alast), (8, LANES)),
    ], axis=0)
    wide = jnp.dot(jnp.concatenate(_split3(stacked), axis=1), expand_ref[...],
                   preferred_element_type=F32)
    decay_in_w = wide[0:lq, :]
    dt_decay_out_w = wide[lq:2 * lq, :]
    chunk_decay_w = wide[2 * lq:2 * lq + 1, :]
    lane = lax.broadcasted_iota(jnp.int32, (lq, pair), 1)
    lo_half = lane < SSM_HEAD_DIM

    for g in range(n_groups):
        sl = slice(g * gw, (g + 1) * gw)
        xs = xbc_ref[:, sl]
        bm = xbc_ref[:, d_inner + g * SSM_D_STATE:d_inner + (g + 1) * SSM_D_STATE]
        cm = xbc_ref[:, d_inner + (n_groups + g) * SSM_D_STATE:
                     d_inner + (n_groups + g + 1) * SSM_D_STATE]
        xs32 = xs.astype(F32)
        cb = lax.dot_general(cm, bm, (((1,), (1,)), ((), ())), preferred_element_type=F32)
        state = state_ref[g]
        y = jnp.dot(cm, state.astype(BF16), preferred_element_type=F32) * decay_in_w[:, sl]
        xw = xs * dt_decay_out_w[:, sl].astype(BF16)
        bm_t = bm.astype(F32).T.astype(BF16)
        state_ref[g] = state * chunk_decay_w[:, sl] + jnp.dot(bm_t, xw, preferred_element_type=F32)

        y_pairs = []
        for hp in range(gw // pair):
            xs_pair = xs[:, hp * pair:(hp + 1) * pair]
            zero = jnp.zeros_like(xs_pair)
            mats = []
            for half in range(2):
                h = (g * gw + hp * pair) // SSM_HEAD_DIM + half
                seg = acum[:, h:h + 1] - acum_t[h:h + 1, :]
                decay = jnp.exp(jnp.where(causal, seg, -jnp.inf))
                mats.append((decay * cb * dt_t[h:h + 1, :]).astype(BF16))
            rhs = jnp.concatenate([jnp.where(lo_half, xs_pair, zero),
                                   jnp.where(lo_half, zero, xs_pair)], axis=0)
            y_pairs.append(jnp.dot(jnp.concatenate(mats, axis=1), rhs, preferred_element_type=F32))
        y = y + jnp.concatenate(y_pairs, axis=1)
        y = y + xs32 * dskip_ref[:, sl]
        zg = z_ref[:, sl].astype(F32)
        y = y * (zg * _sigmoid(zg))
        ms = jnp.mean(y * y, axis=-1, keepdims=True)
        o_ref[:, sl] = (y * lax.rsqrt(ms + RMS_EPS) * nw_ref[:, sl]).astype(o_ref.dtype)


def ssd_mixer_core(xbc, z, dt, a_row, dskip_w, norm_w, expand, bsz, seq):
    t, d_inner = z.shape
    n_chunks = seq // SSM_CHUNK
    gw = d_inner // SSM_N_GROUPS
    row = lambda b, c: (b * n_chunks + c, 0)
    fixed = lambda b, c: (0, 0)
    return pl.pallas_call(
        functools.partial(_ssd_kernel, d_inner=d_inner),
        grid=(bsz, n_chunks),
        in_specs=[pl.BlockSpec((SSM_CHUNK, xbc.shape[1]), row),
                  pl.BlockSpec((SSM_CHUNK, d_inner), row),
                  pl.BlockSpec((SSM_CHUNK, LANES), row),
                  pl.BlockSpec((1, LANES), fixed),
                  pl.BlockSpec((1, d_inner), fixed),
                  pl.BlockSpec((1, d_inner), fixed),
                  pl.BlockSpec((3 * LANES, d_inner), fixed)],
        out_specs=pl.BlockSpec((SSM_CHUNK, d_inner), row),
        out_shape=jax.ShapeDtypeStruct((t, d_inner), BF16),
        scratch_shapes=[pltpu.VMEM((SSM_N_GROUPS, SSM_D_STATE, gw), F32)],
        compiler_params=_params("parallel", "arbitrary"),
        name="ssd_mixer_core",
    )(xbc, z, dt, a_row, dskip_w, norm_w, expand)


def _ffn_kernel(h_ref, x_ref, wg_ref, wu_ref, wd_ref, o_ref):
    @pl.when(pl.program_id(1) == 0)
    def _():
        o_ref[...] = x_ref[...]

    h = h_ref[...]
    gate = jnp.dot(h, wg_ref[...], preferred_element_type=F32)
    up = jnp.dot(h, wu_ref[...], preferred_element_type=F32)
    act = (gate * _sigmoid(gate) * up).astype(BF16)
    o_ref[...] += jnp.dot(act, wd_ref[...], preferred_element_type=F32)


def swiglu_ffn_residual(h, x, w_gate, w_up, w_down, tm=512, tf=512):
    m, d = h.shape
    d_ff = w_gate.shape[1]
    return pl.pallas_call(
        _ffn_kernel,
        grid=(m // tm, d_ff // tf),
        in_specs=[pl.BlockSpec((tm, d), lambda i, f: (i, 0)),
                  pl.BlockSpec((tm, d), lambda i, f: (i, 0)),
                  pl.BlockSpec((d, tf), lambda i, f: (0, f)),
                  pl.BlockSpec((d, tf), lambda i, f: (0, f)),
                  pl.BlockSpec((tf, d), lambda i, f: (f, 0))],
        out_specs=pl.BlockSpec((tm, d), lambda i, f: (i, 0)),
        out_shape=jax.ShapeDtypeStruct((m, d), F32),
        compiler_params=_params("parallel", "arbitrary"),
        name="swiglu_ffn",
    )(h, x, w_gate, w_up, w_down)


def _qk_kernel(a_ref, w_ref, gain_ref, cos_ref, sin_ref, o_ref, *, rows):
    w = w_ref[...]
    gain = gain_ref[...]
    hd = ATTN_HEAD_DIM
    for c in range(a_ref.shape[0] // rows):
        rs = slice(c * rows, (c + 1) * rows)
        acc = jnp.dot(a_ref[rs, :], w, preferred_element_type=F32)
        cos = cos_ref[rs, :]
        sin = sin_ref[rs, :]
        for h in range(acc.shape[1] // hd):
            t = acc[:, h * hd:(h + 1) * hd]
            ms = jnp.mean(t * t, axis=-1, keepdims=True)
            tn = t * lax.rsqrt(ms + RMS_EPS) * gain
            rot = tn * cos + pltpu.roll(tn, hd // 2, axis=1) * sin
            o_ref[rs, h * hd:(h + 1) * hd] = rot.astype(o_ref.dtype)


def qk_projection(a, w, col0, n, gains, cos, sin, seq, tm=1024, rows=256):
    m, k = a.shape
    tiles = seq // tm
    j0 = col0 // n
    assert j0 * n == col0
    return pl.pallas_call(
        functools.partial(_qk_kernel, rows=rows),
        grid=(m // tm, 2),
        in_specs=[pl.BlockSpec((tm, k), lambda i, j: (i, 0)),
                  pl.BlockSpec((k, n), lambda i, j: (0, j0 + j)),
                  pl.BlockSpec((None, 1, ATTN_HEAD_DIM), lambda i, j: (j, 0, 0)),
                  pl.BlockSpec((tm, ATTN_HEAD_DIM), lambda i, j: (i % tiles, 0)),
                  pl.BlockSpec((tm, ATTN_HEAD_DIM), lambda i, j: (i % tiles, 0))],
        out_specs=pl.BlockSpec((None, tm, n), lambda i, j: (j, i, 0)),
        out_shape=jax.ShapeDtypeStruct((2, m, n), BF16),
        compiler_params=_params("parallel", "arbitrary"),
        name="qk_projection",
    )(a, w, gains, cos, sin)


def _attn_kernel(*refs, seq):
    n_groups = len(ATTN_PATTERNS)
    qkv = refs[:3 * n_groups]
    o_ref = refs[3 * n_groups]
    out_scr, lse_scr, vext_all = refs[3 * n_groups + 1:]
    blk = ATTN_BLOCK
    hd = ATTN_HEAD_DIM
    scale = hd ** -0.5
    qi = lax.broadcasted_iota(jnp.int32, (blk, blk), 0)
    kj = lax.broadcasted_iota(jnp.int32, (blk, blk), 1)
    own_ok = kj <= qi
    prev_ok = kj >= qi
    contract_last = (((1,), (1,)), ((), ()))
    for g in range(n_groups):
        vext_all[g, :, hd:] = jnp.ones((seq, hd), BF16)

    def attend(q_ref, k_ref, vext_scr, blocks):
        qs = [q_ref[p0:p0 + blk, :] for p0, _ in blocks]
        s_own = [lax.dot_general(q, k_ref[p0:p0 + blk, :], contract_last, preferred_element_type=F32)
                 for q, (p0, _) in zip(qs, blocks)]
        s_prev = [lax.dot_general(q, k_ref[p0 - blk:p0, :], contract_last, preferred_element_type=F32)
                  if has_prev else None for q, (p0, has_prev) in zip(qs, blocks)]
        results = []
        for so, sp, (p0, has_prev) in zip(s_own, s_prev, blocks):
            so = jnp.where(own_ok, so * scale, -jnp.inf)
            if has_prev:
                sp = jnp.where(prev_ok, sp * scale, -jnp.inf)
                m = jnp.max(jnp.maximum(so, sp), axis=-1, keepdims=True)
            else:
                m = jnp.max(so, axis=-1, keepdims=True)
            ext = jnp.dot(jnp.exp(so - m).astype(BF16), vext_scr[p0:p0 + blk, :],
                          preferred_element_type=F32)
            if has_prev:
                ext = ext + jnp.dot(jnp.exp(sp - m).astype(BF16), vext_scr[p0 - blk:p0, :],
                                    preferred_element_type=F32)
            denom = ext[:, hd:]
            results.append((ext[:, :hd] / denom, m + jnp.log(denom)))
        return results

    batch = 4
    for g, (window, dil) in enumerate(ATTN_PATTERNS):
        assert window // dil == blk
        q_ref, k_ref, v_ref = qkv[3 * g:3 * g + 3]
        vext_scr = vext_all.at[g]
        vext_scr[:, :hd] = v_ref[...]
        nb = seq // dil // blk
        blocks = [((r * nb + i) * blk, i > 0, i * blk * dil + r) for r in range(dil) for i in range(nb)]
        for b0 in range(0, len(blocks), batch):
            chunk = blocks[b0:b0 + batch]
            for (out, lse), (p0, _, t0) in zip(attend(q_ref, k_ref, vext_scr, [c[:2] for c in chunk]), chunk):
                dst = pl.ds(p0, blk) if dil == 1 else pl.ds(t0, blk, stride=dil)
                out_scr[g, dst, :] = out
                lse_scr[g, dst, :] = lse

    lse_all = [lse_scr[g] for g in range(n_groups)]
    top = functools.reduce(jnp.maximum, lse_all)
    num = jnp.zeros((seq, hd), F32)
    den = jnp.zeros((seq, hd), F32)
    for g in range(n_groups):
        wgt = jnp.exp(lse_all[g] - top)
        num = num + wgt * out_scr[g]
        den = den + wgt
    o_ref[...] = (num / den).astype(o_ref.dtype)


def dilated_attention(qk_groups, v_groups, bsz, seq):
    t, n = v_groups[0].shape
    heads = n // ATTN_HEAD_DIM
    in_specs, args = [], []
    for qk, v in zip(qk_groups, v_groups):
        for which in range(2):
            in_specs.append(pl.BlockSpec((None, seq, ATTN_HEAD_DIM),
                                         lambda b, h, which=which: (which, b, h)))
            args.append(qk)
        in_specs.append(pl.BlockSpec((seq, ATTN_HEAD_DIM), lambda b, h: (b, h)))
        args.append(v)
    n_groups = len(v_groups)
    return pl.pallas_call(
        functools.partial(_attn_kernel, seq=seq),
        grid=(bsz, heads),
        in_specs=in_specs,
        out_specs=pl.BlockSpec((seq, ATTN_HEAD_DIM), lambda b, h: (b, h)),
        out_shape=jax.ShapeDtypeStruct((t, n), BF16),
        scratch_shapes=[pltpu.VMEM((n_groups, seq, ATTN_HEAD_DIM), F32),
                        pltpu.VMEM((n_groups, seq, ATTN_HEAD_DIM), F32),
                        pltpu.VMEM((n_groups, seq, 2 * ATTN_HEAD_DIM), BF16)],
        compiler_params=_params("parallel", "parallel"),
        name="dilated_attention",
    )(*args)


def _dilate_rows(table, dil):
    seq, c = table.shape
    return table.reshape(seq // dil, dil, c).transpose(1, 0, 2).reshape(seq, c)


def _rope_tables(seq):
    half = ATTN_HEAD_DIM // 2
    inv_freq = ROPE_THETA ** (-jnp.arange(half, dtype=F32) / half)
    ang = jnp.arange(seq, dtype=F32)[:, None] * inv_freq[None, :]
    cos, sin = jnp.cos(ang), jnp.sin(ang)
    return jnp.concatenate([cos, cos], axis=1), jnp.concatenate([-sin, sin], axis=1)


def mamba2_layer(x, h, bsz, seq, w_in, conv_w, conv_b, dt_bias, a_log, d_skip, norm_w, w_out, next_gain):
    d_inner = w_out.shape[0]
    n_heads = d_inner // SSM_HEAD_DIM
    conv_dim = conv_w.shape[1]
    d_model = w_in.shape[0]
    w_in_bf = w_in.astype(BF16)
    w_dt = jnp.zeros((d_model, LANES), BF16).at[:, :n_heads].set(w_in_bf[:, d_inner + conv_dim:])
    dt_bias_row = jnp.zeros((1, LANES), F32).at[0, :n_heads].set(dt_bias.astype(F32))
    a_row = jnp.zeros((1, LANES), F32).at[0, :n_heads].set(-jnp.exp(a_log.astype(F32)))
    dskip_w = jnp.repeat(d_skip.astype(F32), SSM_HEAD_DIM).reshape(1, d_inner)
    head_of_channel = jnp.arange(d_inner) // SSM_HEAD_DIM
    expand = (jnp.arange(LANES)[:, None] == head_of_channel[None, :]).astype(BF16)
    expand = jnp.concatenate([expand, expand, expand], axis=0)

    z = matmul(h, w_in_bf, BF16, col0=0, n=d_inner)
    xbc = conv_projection(h, w_in_bf, d_inner, conv_dim, conv_w.astype(F32), conv_b.astype(F32), seq)
    dt = dt_projection(h, w_dt, dt_bias_row)
    y = ssd_mixer_core(xbc, z, dt, a_row, dskip_w, norm_w.astype(F32).reshape(1, d_inner), expand,
                       bsz, seq)
    return matmul_residual_norm(y, w_out.astype(BF16), x, next_gain)


def attention_layer(x, hs, bsz, seq, w_qkv, q_norm, k_norm, w_out, next_gain):
    d_model = w_qkv.shape[0]
    n_groups = len(ATTN_PATTERNS)
    n = ATTN_GROUP_HEADS * ATTN_HEAD_DIM
    w = w_qkv.astype(BF16)
    cos, sin = _rope_tables(seq)
    qk_groups, v_groups = [], []
    for g, (_, dil) in enumerate(ATTN_PATTERNS):
        gains = jnp.stack([q_norm[g], k_norm[g]]).astype(F32).reshape(2, 1, ATTN_HEAD_DIM)
        col0 = g * 3 * n
        qk_groups.append(qk_projection(hs[g], w, col0, n, gains, _dilate_rows(cos, dil),
                                       _dilate_rows(sin, dil), seq))
        v_groups.append(matmul(hs[g], w, BF16, col0=col0 + 2 * n, n=n))
    merged = dilated_attention(qk_groups, v_groups, bsz, seq)
    return matmul_residual_norm(merged, w_out.astype(BF16), x, next_gain)


def kernel(x, mix_norm, ffn_norm, ssm_w_in, ssm_conv_w, ssm_conv_b, ssm_dt_bias, ssm_a_log, ssm_d, ssm_norm, ssm_w_out, attn_w_qkv, attn_q_norm, attn_k_norm, attn_w_out, ffn_w_gate, ffn_w_up, ffn_w_down):
    bsz, seq, d_model = x.shape
    depth = mix_norm.shape[0]
    xt = x.reshape(bsz * seq, d_model).astype(F32)
    for layer in range(depth):
        j = layer // 2
        if layer % 2 == 0:
            h = rmsnorm(xt, mix_norm[layer].astype(F32))
            xt, h = mamba2_layer(xt, h, bsz, seq, ssm_w_in[j], ssm_conv_w[j], ssm_conv_b[j], ssm_dt_bias[j],
                                 ssm_a_log[j], ssm_d[j], ssm_norm[j], ssm_w_out[j], ffn_norm[layer].astype(F32))
        else:
            hs = rmsnorm_dilated(xt, mix_norm[layer].astype(F32), bsz, seq)
            xt, h = attention_layer(xt, hs, bsz, seq, attn_w_qkv[j], attn_q_norm[j], attn_k_norm[j],
                                    attn_w_out[j], ffn_norm[layer].astype(F32))
        xt = swiglu_ffn_residual(h, xt, ffn_w_gate[layer].astype(BF16), ffn_w_up[layer].astype(BF16),
                                 ffn_w_down[layer].astype(BF16))
    return xt.reshape(bsz, seq, d_model).astype(x.dtype)
```

```python
import functools
import math

import jax
import jax.numpy as jnp
from jax import lax
from jax.experimental import pallas as pl
from jax.experimental.pallas import tpu as pltpu

F32 = jnp.float32
BF16 = jnp.bfloat16

RMS_EPS = 1e-6
LANES = 128
VMEM_LIMIT_BYTES = 56 * 1024 * 1024

SSM_HEAD_DIM = 64
SSM_D_STATE = 128
SSM_N_GROUPS = 8
SSM_CONV_WIDTH = 4
SSM_CHUNK = 128

ATTN_HEAD_DIM = 128
ATTN_GROUP_HEADS = 8
ATTN_PATTERNS = ((128, 1), (512, 4), (2048, 16))
ATTN_BLOCK = 128
ROPE_THETA = 10000.0


def _params(*sem):
    return pltpu.CompilerParams(dimension_semantics=sem, vmem_limit_bytes=VMEM_LIMIT_BYTES)


def _sigmoid(v):
    return 1.0 / (1.0 + jnp.exp(-v))


def _split3(v):
    hi = v.astype(BF16)
    r1 = v - hi.astype(F32)
    mid = r1.astype(BF16)
    lo = (r1 - mid.astype(F32)).astype(BF16)
    return hi, mid, lo


def _rmsnorm_kernel(x_ref, g_ref, o_ref):
    x = x_ref[...]
    ms = jnp.mean(x * x, axis=-1, keepdims=True)
    o_ref[...] = (x * lax.rsqrt(ms + RMS_EPS) * g_ref[...]).astype(o_ref.dtype)


def rmsnorm(x, gain, tm=512):
    t, d = x.shape
    return pl.pallas_call(
        _rmsnorm_kernel,
        grid=(t // tm,),
        in_specs=[pl.BlockSpec((tm, d), lambda i: (i, 0)),
                  pl.BlockSpec((1, d), lambda i: (0, 0))],
        out_specs=pl.BlockSpec((tm, d), lambda i: (i, 0)),
        out_shape=jax.ShapeDtypeStruct((t, d), BF16),
        compiler_params=_params("parallel"),
        name="rmsnorm",
    )(x, gain.reshape(1, d))


def _rmsnorm_dilated_kernel(x_ref, g_ref, o1_ref, o4_ref, o16_ref, scr):
    x = x_ref[...]
    ms = jnp.mean(x * x, axis=-1, keepdims=True)
    y = x * lax.rsqrt(ms + RMS_EPS) * g_ref[...]
    o1_ref[...] = y.astype(o1_ref.dtype)
    n_slabs = scr.shape[0]
    for s in range(n_slabs):
        scr[s] = y[:, s * LANES:(s + 1) * LANES]
    for o_ref in (o4_ref, o16_ref):
        dil, rows = o_ref.shape[0], o_ref.shape[1]
        for r in range(dil):
            for s in range(n_slabs):
                o_ref[r, :, s * LANES:(s + 1) * LANES] = (
                    scr[s, pl.ds(r, rows, stride=dil), :].astype(o_ref.dtype))


def rmsnorm_dilated(x, gain, bsz, seq, tm=256):
    t, d = x.shape
    tiles = seq // tm
    d4, d16 = ATTN_PATTERNS[1][1], ATTN_PATTERNS[2][1]
    o1, o4, o16 = pl.pallas_call(
        _rmsnorm_dilated_kernel,
        grid=(bsz, tiles),
        in_specs=[pl.BlockSpec((tm, d), lambda b, i: (b * tiles + i, 0)),
                  pl.BlockSpec((1, d), lambda b, i: (0, 0))],
        out_specs=[pl.BlockSpec((tm, d), lambda b, i: (b * tiles + i, 0)),
                   pl.BlockSpec((None, d4, tm // d4, d), lambda b, i: (b, 0, i, 0)),
                   pl.BlockSpec((None, d16, tm // d16, d), lambda b, i: (b, 0, i, 0))],
        out_shape=[jax.ShapeDtypeStruct((t, d), BF16),
                   jax.ShapeDtypeStruct((bsz, d4, seq // d4, d), BF16),
                   jax.ShapeDtypeStruct((bsz, d16, seq // d16, d), BF16)],
        scratch_shapes=[pltpu.VMEM((d // LANES, tm, LANES), F32)],
        compiler_params=_params("parallel", "parallel"),
        name="rmsnorm_dilated",
    )(x, gain.reshape(1, d))
    return o1, o4.reshape(t, d), o16.reshape(t, d)


def _mm_kernel(a_ref, w_ref, o_ref):
    o_ref[...] = jnp.dot(a_ref[...], w_ref[...], preferred_element_type=F32).astype(o_ref.dtype)


def matmul(a, w, out_dtype, col0=0, n=None, tm=1024, tn=1024):
    m, k = a.shape
    n = w.shape[1] if n is None else n
    j0 = col0 // tn
    assert j0 * tn == col0
    return pl.pallas_call(
        _mm_kernel,
        grid=(m // tm, n // tn),
        in_specs=[pl.BlockSpec((tm, k), lambda i, j: (i, 0)),
                  pl.BlockSpec((k, tn), lambda i, j: (0, j0 + j))],
        out_specs=pl.BlockSpec((tm, tn), lambda i, j: (i, j)),
        out_shape=jax.ShapeDtypeStruct((m, n), out_dtype),
        compiler_params=_params("parallel", "parallel"),
        name="matmul",
    )(a, w)


def _mm_res_norm_kernel(a_ref, w_ref, r_ref, g_ref, o_ref, h_ref, *, n_k):
    k = pl.program_id(1)

    @pl.when(k == 0)
    def _():
        o_ref[...] = r_ref[...]

    o_ref[...] += jnp.dot(a_ref[...], w_ref[...], preferred_element_type=F32)

    @pl.when(k == n_k - 1)
    def _():
        x = o_ref[...]
        ms = jnp.mean(x * x, axis=-1, keepdims=True)
        h_ref[...] = (x * lax.rsqrt(ms + RMS_EPS) * g_ref[...]).astype(h_ref.dtype)


def matmul_residual_norm(a, w, res, gain, tm=256, tk=None):
    m, k = a.shape
    n = w.shape[1]
    tk = k if tk is None else tk
    n_k = k // tk
    return pl.pallas_call(
        functools.partial(_mm_res_norm_kernel, n_k=n_k),
        grid=(m // tm, n_k),
        in_specs=[pl.BlockSpec((tm, tk), lambda i, kk: (i, kk)),
                  pl.BlockSpec((tk, n), lambda i, kk: (kk, 0)),
                  pl.BlockSpec((tm, n), lambda i, kk: (i, 0)),
                  pl.BlockSpec((1, n), lambda i, kk: (0, 0))],
        out_specs=[pl.BlockSpec((tm, n), lambda i, kk: (i, 0)),
                   pl.BlockSpec((tm, n), lambda i, kk: (i, 0))],
        out_shape=[jax.ShapeDtypeStruct((m, n), F32), jax.ShapeDtypeStruct((m, n), BF16)],
        compiler_params=_params("parallel", "arbitrary"),
        name="matmul_residual_norm",
    )(a, w, res, gain.reshape(1, n))


def _dt_kernel(a_ref, w_ref, b_ref, o_ref):
    raw = jnp.dot(a_ref[...], w_ref[...], preferred_element_type=F32) + b_ref[...]
    o_ref[...] = jnp.maximum(raw, 0.0) + jnp.log(1.0 + jnp.exp(-jnp.abs(raw)))


def dt_projection(a, w, bias, tm=1024):
    m, k = a.shape
    n = w.shape[1]
    return pl.pallas_call(
        _dt_kernel,
        grid=(m // tm,),
        in_specs=[pl.BlockSpec((tm, k), lambda i: (i, 0)),
                  pl.BlockSpec((k, n), lambda i: (0, 0)),
                  pl.BlockSpec((1, n), lambda i: (0, 0))],
        out_specs=pl.BlockSpec((tm, n), lambda i: (i, 0)),
        out_shape=jax.ShapeDtypeStruct((m, n), F32),
        compiler_params=_params("parallel"),
        name="dt_projection",
    )(a, w, bias)


def _conv_proj_kernel(a_ref, w_ref, cw_ref, cb_ref, o_ref, acc0, acc1, tail_scr, *, tiles_per_seq, rows=64):
    i = pl.program_id(1)
    tm = a_ref.shape[0]
    halo = tail_scr.shape[0]

    @pl.when(lax.rem(i, tiles_per_seq) == 0)
    def _():
        tail_scr[...] = jnp.zeros(tail_scr.shape, F32)

    def project(dst, cs):
        dst[...] = jnp.dot(a_ref[...], w_ref[:, cs], preferred_element_type=F32)

    def conv_silu(src, cs):
        cw = cw_ref[:, cs]
        bias = cb_ref[:, cs]
        zero = pl.multiple_of(jnp.minimum(i, 0) * 8, 8)
        for c in range(tm // rows):
            r0 = c * rows
            head = tail_scr[:, cs] if c == 0 else src[pl.ds(r0 - halo + zero, halo), :]
            u = src[pl.ds(r0 + zero, rows), :]
            ext = jnp.concatenate([head, u], axis=0)
            y = bias + cw[SSM_CONV_WIDTH - 1:SSM_CONV_WIDTH, :] * u
            for k in range(SSM_CONV_WIDTH - 1):
                shift = SSM_CONV_WIDTH - 1 - k
                y = y + cw[k:k + 1, :] * pltpu.roll(ext, shift, axis=0)[halo:, :]
            o_ref[r0:r0 + rows, cs] = (y * _sigmoid(y)).astype(o_ref.dtype)
        tail_scr[:, cs] = src[tm - halo:, :]

    accs = (acc0, acc1)
    width = acc0.shape[1]
    n_chunks = o_ref.shape[1] // width
    for nc in range(n_chunks):
        cs = slice(nc * width, (nc + 1) * width)
        project(accs[nc % 2], cs)
        if nc > 0:
            conv_silu(accs[(nc - 1) % 2], slice((nc - 1) * width, nc * width))
    conv_silu(accs[(n_chunks - 1) % 2], slice((n_chunks - 1) * width, n_chunks * width))


def conv_projection(a, w, col0, n, conv_w, conv_b, seq, tm=1024, tn=2048, chunk=256):
    m, k = a.shape
    j0 = col0 // tn
    assert j0 * tn == col0
    return pl.pallas_call(
        functools.partial(_conv_proj_kernel, tiles_per_seq=seq // tm),
        grid=(n // tn, m // tm),
        in_specs=[pl.BlockSpec((tm, k), lambda j, i: (i, 0)),
                  pl.BlockSpec((k, tn), lambda j, i: (0, j0 + j)),
                  pl.BlockSpec((SSM_CONV_WIDTH, tn), lambda j, i: (0, j)),
                  pl.BlockSpec((1, tn), lambda j, i: (0, j))],
        out_specs=pl.BlockSpec((tm, tn), lambda j, i: (i, j)),
        out_shape=jax.ShapeDtypeStruct((m, n), BF16),
        scratch_shapes=[pltpu.VMEM((tm, chunk), F32), pltpu.VMEM((tm, chunk), F32),
                        pltpu.VMEM((8, tn), F32)],
        compiler_params=_params("parallel", "arbitrary"),
        name="conv_projection",
    )(a, w, conv_w, conv_b.reshape(1, n))


def _ssd_kernel(xbc_ref, z_ref, dt_ref, a_ref, dskip_ref, nw_ref, expand_ref, o_ref, state_ref,
                *, d_inner):
    c = pl.program_id(1)
    n_groups = SSM_N_GROUPS
    gw = d_inner // n_groups
    pair = 2 * SSM_HEAD_DIM
    lq = SSM_CHUNK

    @pl.when(c == 0)
    def _():
        state_ref[...] = jnp.zeros(state_ref.shape, F32)

    dt = dt_ref[...]
    a = dt * a_ref[...]
    row_i = lax.broadcasted_iota(jnp.int32, (lq, lq), 0)
    col_i = lax.broadcasted_iota(jnp.int32, (lq, lq), 1)
    causal = row_i >= col_i
    tril = jnp.where(causal, 1.0, 0.0).astype(BF16)
    tril3 = jnp.concatenate([tril, tril, tril], axis=1)
    acum = jnp.dot(tril3, jnp.concatenate(_split3(a), axis=0),
                   preferred_element_type=F32)
    acum_t = acum.T
    dt_t = dt.T
    alast = acum[lq - 1:lq, :]
    stacked = jnp.concatenate([
        jnp.exp(acum),
        dt * jnp.exp(alast - acum),
        jnp.broadcast_to(jnp.exp(alast), (8, LANES)),
    ], axis=0)
    wide = jnp.dot(jnp.concatenate(_split3(stacked), axis=1), expand_ref[...],
                   preferred_element_type=F32)
    decay_in_w = wide[0:lq, :]
    dt_decay_out_w = wide[lq:2 * lq, :]
    chunk_decay_w = wide[2 * lq:2 * lq + 1, :]
    lane = lax.broadcasted_iota(jnp.int32, (lq, pair), 1)
    lo_half = lane < SSM_HEAD_DIM

    for g in range(n_groups):
        sl = slice(g * gw, (g + 1) * gw)
        xs = xbc_ref[:, sl]
        bm = xbc_ref[:, d_inner + g * SSM_D_STATE:d_inner + (g + 1) * SSM_D_STATE]
        cm = xbc_ref[:, d_inner + (n_groups + g) * SSM_D_STATE:
                     d_inner + (n_groups + g + 1) * SSM_D_STATE]
        xs32 = xs.astype(F32)
        cb = lax.dot_general(cm, bm, (((1,), (1,)), ((), ())), preferred_element_type=F32)
        state = state_ref[g]
        y = jnp.dot(cm, state.astype(BF16), preferred_element_type=F32) * decay_in_w[:, sl]
        xw = xs * dt_decay_out_w[:, sl].astype(BF16)
        bm_t = bm.astype(F32).T.astype(BF16)
        state_ref[g] = state * chunk_decay_w[:, sl] + jnp.dot(bm_t, xw, preferred_element_type=F32)

        y_pairs = []
        for hp in range(gw // pair):
            xs_pair = xs[:, hp * pair:(hp + 1) * pair]
            zero = jnp.zeros_like(xs_pair)
            mats = []
            for half in range(2):
                h = (g * gw + hp * pair) // SSM_HEAD_DIM + half
                seg = acum[:, h:h + 1] - acum_t[h:h + 1, :]
                decay = jnp.exp(jnp.where(causal, seg, -jnp.inf))
                mats.append((decay * cb * dt_t[h:h + 1, :]).astype(BF16))
            rhs = jnp.concatenate([jnp.where(lo_half, xs_pair, zero),
                                   jnp.where(lo_half, zero, xs_pair)], axis=0)
            y_pairs.append(jnp.dot(jnp.concatenate(mats, axis=1), rhs, preferred_element_type=F32))
        y = y + jnp.concatenate(y_pairs, axis=1)
        y = y + xs32 * dskip_ref[:, sl]
        zg = z_ref[:, sl].astype(F32)
        y = y * (zg * _sigmoid(zg))
        ms = jnp.mean(y * y, axis=-1, keepdims=True)
        o_ref[:, sl] = (y * lax.rsqrt(ms + RMS_EPS) * nw_ref[:, sl]).astype(o_ref.dtype)


def ssd_mixer_core(xbc, z, dt, a_row, dskip_w, norm_w, expand, bsz, seq):
    t, d_inner = z.shape
    n_chunks = seq // SSM_CHUNK
    gw = d_inner // SSM_N_GROUPS
    row = lambda b, c: (b * n_chunks + c, 0)
    fixed = lambda b, c: (0, 0)
    return pl.pallas_call(
        functools.partial(_ssd_kernel, d_inner=d_inner),
        grid=(bsz, n_chunks),
        in_specs=[pl.BlockSpec((SSM_CHUNK, xbc.shape[1]), row),
                  pl.BlockSpec((SSM_CHUNK, d_inner), row),
                  pl.BlockSpec((SSM_CHUNK, LANES), row),
                  pl.BlockSpec((1, LANES), fixed),
                  pl.BlockSpec((1, d_inner), fixed),
                  pl.BlockSpec((1, d_inner), fixed),
                  pl.BlockSpec((3 * LANES, d_inner), fixed)],
        out_specs=pl.BlockSpec((SSM_CHUNK, d_inner), row),
        out_shape=jax.ShapeDtypeStruct((t, d_inner), BF16),
        scratch_shapes=[pltpu.VMEM((SSM_N_GROUPS, SSM_D_STATE, gw), F32)],
        compiler_params=_params("parallel", "arbitrary"),
        name="ssd_mixer_core",
    )(xbc, z, dt, a_row, dskip_w, norm_w, expand)


def _ffn_kernel(h_ref, x_ref, wg_ref, wu_ref, wd_ref, o_ref):
    @pl.when(pl.program_id(1) == 0)
    def _():
        o_ref[...] = x_ref[...]

    h = h_ref[...]
    gate = jnp.dot(h, wg_ref[...], preferred_element_type=F32)
    up = jnp.dot(h, wu_ref[...], preferred_element_type=F32)
    act = (gate * _sigmoid(gate) * up).astype(BF16)
    o_ref[...] += jnp.dot(act, wd_ref[...], preferred_element_type=F32)


def swiglu_ffn_residual(h, x, w_gate, w_up, w_down, layer, tm=512, tf=512):
    m, d = h.shape
    d_ff = w_gate.shape[2]
    return pl.pallas_call(
        _ffn_kernel,
        grid=(m // tm, d_ff // tf),
        in_specs=[pl.BlockSpec((tm, d), lambda i, f: (i, 0)),
                  pl.BlockSpec((tm, d), lambda i, f: (i, 0)),
                  pl.BlockSpec((None, d, tf), lambda i, f: (layer, 0, f)),
                  pl.BlockSpec((None, d, tf), lambda i, f: (layer, 0, f)),
                  pl.BlockSpec((None, tf, d), lambda i, f: (layer, f, 0))],
        out_specs=pl.BlockSpec((tm, d), lambda i, f: (i, 0)),
        out_shape=jax.ShapeDtypeStruct((m, d), F32),
        compiler_params=_params("parallel", "arbitrary"),
        name="swiglu_ffn",
    )(h, x, w_gate, w_up, w_down)


def _qk_kernel(a_ref, w_ref, gain_ref, cos_ref, sin_ref, o_ref, *, rows):
    w = w_ref[...]
    gain = gain_ref[...]
    hd = ATTN_HEAD_DIM
    for c in range(a_ref.shape[0] // rows):
        rs = slice(c * rows, (c + 1) * rows)
        acc = jnp.dot(a_ref[rs, :], w, preferred_element_type=F32)
        cos = cos_ref[rs, :]
        sin = sin_ref[rs, :]
        for h in range(acc.shape[1] // hd):
            t = acc[:, h * hd:(h + 1) * hd]
            ms = jnp.mean(t * t, axis=-1, keepdims=True)
            tn = t * lax.rsqrt(ms + RMS_EPS) * gain
            rot = tn * cos + pltpu.roll(tn, hd // 2, axis=1) * sin
            o_ref[rs, h * hd:(h + 1) * hd] = rot.astype(o_ref.dtype)


def qk_projection(a, w, col0, n, gains, cos, sin, seq, tm=1024, rows=256):
    m, k = a.shape
    tiles = seq // tm
    j0 = col0 // n
    assert j0 * n == col0
    return pl.pallas_call(
        functools.partial(_qk_kernel, rows=rows),
        grid=(m // tm, 2),
        in_specs=[pl.BlockSpec((tm, k), lambda i, j: (i, 0)),
                  pl.BlockSpec((k, n), lambda i, j: (0, j0 + j)),
                  pl.BlockSpec((None, 1, ATTN_HEAD_DIM), lambda i, j: (j, 0, 0)),
                  pl.BlockSpec((tm, ATTN_HEAD_DIM), lambda i, j: (i % tiles, 0)),
                  pl.BlockSpec((tm, ATTN_HEAD_DIM), lambda i, j: (i % tiles, 0))],
        out_specs=pl.BlockSpec((None, tm, n), lambda i, j: (j, i, 0)),
        out_shape=jax.ShapeDtypeStruct((2, m, n), BF16),
        compiler_params=_params("parallel", "arbitrary"),
        name="qk_projection",
    )(a, w, gains, cos, sin)


def _attn_kernel(*refs, seq):
    n_groups = len(ATTN_PATTERNS)
    qkv = refs[:3 * n_groups]
    o_ref = refs[3 * n_groups]
    out_scr, lse_scr, vext_all = refs[3 * n_groups + 1:]
    blk = ATTN_BLOCK
    hd = ATTN_HEAD_DIM
    scale = hd ** -0.5
    qi = lax.broadcasted_iota(jnp.int32, (blk, blk), 0)
    kj = lax.broadcasted_iota(jnp.int32, (blk, blk), 1)
    own_ok = kj <= qi
    prev_ok = kj >= qi
    contract_last = (((1,), (1,)), ((), ()))
    for g in range(n_groups):
        vext_all[g, :, hd:] = jnp.ones((seq, hd), BF16)

    def attend(q_ref, k_ref, vext_scr, blocks):
        qs = [q_ref[p0:p0 + blk, :] for p0, _ in blocks]
        s_own = [lax.dot_general(q, k_ref[p0:p0 + blk, :], contract_last, preferred_element_type=F32)
                 for q, (p0, _) in zip(qs, blocks)]
        s_prev = [lax.dot_general(q, k_ref[p0 - blk:p0, :], contract_last, preferred_element_type=F32)
                  if has_prev else None for q, (p0, has_prev) in zip(qs, blocks)]
        results = []
        for so, sp, (p0, has_prev) in zip(s_own, s_prev, blocks):
            so = jnp.where(own_ok, so * scale, -jnp.inf)
            if has_prev:
                sp = jnp.where(prev_ok, sp * scale, -jnp.inf)
                m = jnp.max(jnp.maximum(so, sp), axis=-1, keepdims=True)
            else:
                m = jnp.max(so, axis=-1, keepdims=True)
            ext = jnp.dot(jnp.exp(so - m).astype(BF16), vext_scr[p0:p0 + blk, :],
                          preferred_element_type=F32)
            if has_prev:
                ext = ext + jnp.dot(jnp.exp(sp - m).astype(BF16), vext_scr[p0 - blk:p0, :],
                                    preferred_element_type=F32)
            denom = ext[:, hd:]
            results.append((ext[:, :hd] / denom, m + jnp.log(denom)))
        return results

    batch = seq // blk
    for g, (window, dil) in enumerate(ATTN_PATTERNS):
        assert window // dil == blk
        q_ref, k_ref, v_ref = qkv[3 * g:3 * g + 3]
        vext_scr = vext_all.at[g]
        vext_scr[:, :hd] = v_ref[...]
        nb = seq // dil // blk
        blocks = [((r * nb + i) * blk, i > 0, i * blk * dil + r) for r in range(dil) for i in range(nb)]
        for b0 in range(0, len(blocks), batch):
            chunk = blocks[b0:b0 + batch]
            for (out, lse), (p0, _, t0) in zip(attend(q_ref, k_ref, vext_scr, [c[:2] for c in chunk]), chunk):
                dst = pl.ds(p0, blk) if dil == 1 else pl.ds(t0, blk, stride=dil)
                out_scr[g, dst, :] = out
                lse_scr[g, dst, :] = lse

    lse_all = [lse_scr[g] for g in range(n_groups)]
    top = functools.reduce(jnp.maximum, lse_all)
    num = jnp.zeros((seq, hd), F32)
    den = jnp.zeros((seq, hd), F32)
    for g in range(n_groups):
        wgt = jnp.exp(lse_all[g] - top)
        num = num + wgt * out_scr[g]
        den = den + wgt
    o_ref[...] = (num / den).astype(o_ref.dtype)


def dilated_attention(qk_groups, v_groups, bsz, seq):
    t, n = v_groups[0].shape
    heads = n // ATTN_HEAD_DIM
    in_specs, args = [], []
    for qk, v in zip(qk_groups, v_groups):
        for which in range(2):
            in_specs.append(pl.BlockSpec((None, seq, ATTN_HEAD_DIM),
                                         lambda b, h, which=which: (which, b, h)))
            args.append(qk)
        in_specs.append(pl.BlockSpec((seq, ATTN_HEAD_DIM), lambda b, h: (b, h)))
        args.append(v)
    n_groups = len(v_groups)
    return pl.pallas_call(
        functools.partial(_attn_kernel, seq=seq),
        grid=(bsz, heads),
        in_specs=in_specs,
        out_specs=pl.BlockSpec((seq, ATTN_HEAD_DIM), lambda b, h: (b, h)),
        out_shape=jax.ShapeDtypeStruct((t, n), BF16),
        scratch_shapes=[pltpu.VMEM((n_groups, seq, ATTN_HEAD_DIM), F32),
                        pltpu.VMEM((n_groups, seq, ATTN_HEAD_DIM), F32),
                        pltpu.VMEM((n_groups, seq, 2 * ATTN_HEAD_DIM), BF16)],
        compiler_params=_params("parallel", "parallel"),
        name="dilated_attention",
    )(*args)


def _dilate_rows(table, dil):
    seq, c = table.shape
    return table.reshape(seq // dil, dil, c).transpose(1, 0, 2).reshape(seq, c)


def _rope_tables(seq):
    half = ATTN_HEAD_DIM // 2
    inv_freq = ROPE_THETA ** (-jnp.arange(half, dtype=F32) / half)
    ang = jnp.arange(seq, dtype=F32)[:, None] * inv_freq[None, :]
    cos, sin = jnp.cos(ang), jnp.sin(ang)
    return jnp.concatenate([cos, cos], axis=1), jnp.concatenate([-sin, sin], axis=1)


def mamba2_layer(x, h, bsz, seq, w_in, conv_w, conv_b, dt_bias, a_log, d_skip, norm_w, w_out, next_gain):
    d_inner = w_out.shape[0]
    n_heads = d_inner // SSM_HEAD_DIM
    conv_dim = conv_w.shape[1]
    d_model = w_in.shape[0]
    w_in_bf = w_in.astype(BF16)
    w_dt = jnp.zeros((d_model, LANES), BF16).at[:, :n_heads].set(w_in_bf[:, d_inner + conv_dim:])
    dt_bias_row = jnp.zeros((1, LANES), F32).at[0, :n_heads].set(dt_bias.astype(F32))
    a_row = jnp.zeros((1, LANES), F32).at[0, :n_heads].set(-jnp.exp(a_log.astype(F32)))
    dskip_w = jnp.repeat(d_skip.astype(F32), SSM_HEAD_DIM).reshape(1, d_inner)
    head_of_channel = jnp.arange(d_inner) // SSM_HEAD_DIM
    expand = (jnp.arange(LANES)[:, None] == head_of_channel[None, :]).astype(BF16)
    expand = jnp.concatenate([expand, expand, expand], axis=0)

    z = matmul(h, w_in_bf, BF16, col0=0, n=d_inner)
    xbc = conv_projection(h, w_in_bf, d_inner, conv_dim, conv_w.astype(F32), conv_b.astype(F32), seq)
    dt = dt_projection(h, w_dt, dt_bias_row)
    y = ssd_mixer_core(xbc, z, dt, a_row, dskip_w, norm_w.astype(F32).reshape(1, d_inner), expand,
                       bsz, seq)
    return matmul_residual_norm(y, w_out.astype(BF16), x, next_gain)


def attention_layer(x, hs, bsz, seq, w_qkv, q_norm, k_norm, w_out, next_gain):
    d_model = w_qkv.shape[0]
    n_groups = len(ATTN_PATTERNS)
    n = ATTN_GROUP_HEADS * ATTN_HEAD_DIM
    w = w_qkv.astype(BF16)
    cos, sin = _rope_tables(seq)
    qk_groups, v_groups = [], []
    for g, (_, dil) in enumerate(ATTN_PATTERNS):
        gains = jnp.stack([q_norm[g], k_norm[g]]).astype(F32).reshape(2, 1, ATTN_HEAD_DIM)
        col0 = g * 3 * n
        qk_groups.append(qk_projection(hs[g], w, col0, n, gains, _dilate_rows(cos, dil),
                                       _dilate_rows(sin, dil), seq))
        v_groups.append(matmul(hs[g], w, BF16, col0=col0 + 2 * n, n=n))
    merged = dilated_attention(qk_groups, v_groups, bsz, seq)
    return matmul_residual_norm(merged, w_out.astype(BF16), x, next_gain)


def kernel(x, mix_norm, ffn_norm, ssm_w_in, ssm_conv_w, ssm_conv_b, ssm_dt_bias, ssm_a_log, ssm_d, ssm_norm, ssm_w_out, attn_w_qkv, attn_q_norm, attn_k_norm, attn_w_out, ffn_w_gate, ffn_w_up, ffn_w_down):
    bsz, seq, d_model = x.shape
    depth = mix_norm.shape[0]
    xt = x.reshape(bsz * seq, d_model).astype(F32)
    w_gate, w_up, w_down = ffn_w_gate.astype(BF16), ffn_w_up.astype(BF16), ffn_w_down.astype(BF16)
    for layer in range(depth):
        j = layer // 2
        if layer % 2 == 0:
            h = rmsnorm(xt, mix_norm[layer].astype(F32))
            xt, h = mamba2_layer(xt, h, bsz, seq, ssm_w_in[j], ssm_conv_w[j], ssm_conv_b[j], ssm_dt_bias[j],
                                 ssm_a_log[j], ssm_d[j], ssm_norm[j], ssm_w_out[j], ffn_norm[layer].astype(F32))
        else:
            hs = rmsnorm_dilated(xt, mix_norm[layer].astype(F32), bsz, seq)
            xt, h = attention_layer(xt, hs, bsz, seq, attn_w_qkv[j], attn_q_norm[j], attn_k_norm[j],
                                    attn_w_out[j], ffn_norm[layer].astype(F32))
        xt = swiglu_ffn_residual(h, xt, w_gate, w_up, w_down, layer)
    return xt.reshape(bsz, seq, d_model).astype(x.dtype)
```

```python
import functools
import math

import jax
import jax.numpy as jnp
from jax import lax
from jax.experimental import pallas as pl
from jax.experimental.pallas import tpu as pltpu

F32 = jnp.float32
BF16 = jnp.bfloat16

RMS_EPS = 1e-6
LANES = 128
VMEM_LIMIT_BYTES = 56 * 1024 * 1024

SSM_HEAD_DIM = 64
SSM_D_STATE = 128
SSM_N_GROUPS = 8
SSM_CONV_WIDTH = 4
SSM_CHUNK = 128

ATTN_HEAD_DIM = 128
ATTN_GROUP_HEADS = 8
ATTN_PATTERNS = ((128, 1), (512, 4), (2048, 16))
ATTN_BLOCK = 128
ROPE_THETA = 10000.0


def _params(*sem):
    return pltpu.CompilerParams(dimension_semantics=sem, vmem_limit_bytes=VMEM_LIMIT_BYTES)


def _sigmoid(v):
    return 1.0 / (1.0 + jnp.exp(-v))


def _split3(v):
    hi = v.astype(BF16)
    r1 = v - hi.astype(F32)
    mid = r1.astype(BF16)
    lo = (r1 - mid.astype(F32)).astype(BF16)
    return hi, mid, lo


def _rmsnorm_kernel(x_ref, g_ref, o_ref):
    x = x_ref[...]
    ms = jnp.mean(x * x, axis=-1, keepdims=True)
    o_ref[...] = (x * lax.rsqrt(ms + RMS_EPS) * g_ref[...]).astype(o_ref.dtype)


def rmsnorm(x, gain, tm=512):
    t, d = x.shape
    return pl.pallas_call(
        _rmsnorm_kernel,
        grid=(t // tm,),
        in_specs=[pl.BlockSpec((tm, d), lambda i: (i, 0)),
                  pl.BlockSpec((1, d), lambda i: (0, 0))],
        out_specs=pl.BlockSpec((tm, d), lambda i: (i, 0)),
        out_shape=jax.ShapeDtypeStruct((t, d), BF16),
        compiler_params=_params("parallel"),
        name="rmsnorm",
    )(x, gain.reshape(1, d))


def _rmsnorm_dilated_kernel(x_ref, g_ref, o1_ref, o4_ref, o16_ref, scr):
    x = x_ref[...]
    ms = jnp.mean(x * x, axis=-1, keepdims=True)
    y = x * lax.rsqrt(ms + RMS_EPS) * g_ref[...]
    o1_ref[...] = y.astype(o1_ref.dtype)
    n_slabs = scr.shape[0]
    for s in range(n_slabs):
        scr[s] = y[:, s * LANES:(s + 1) * LANES]
    for o_ref in (o4_ref, o16_ref):
        dil, rows = o_ref.shape[0], o_ref.shape[1]
        for r in range(dil):
            for s in range(n_slabs):
                o_ref[r, :, s * LANES:(s + 1) * LANES] = (
                    scr[s, pl.ds(r, rows, stride=dil), :].astype(o_ref.dtype))


def rmsnorm_dilated(x, gain, bsz, seq, tm=256):
    t, d = x.shape
    tiles = seq // tm
    d4, d16 = ATTN_PATTERNS[1][1], ATTN_PATTERNS[2][1]
    o1, o4, o16 = pl.pallas_call(
        _rmsnorm_dilated_kernel,
        grid=(bsz, tiles),
        in_specs=[pl.BlockSpec((tm, d), lambda b, i: (b * tiles + i, 0)),
                  pl.BlockSpec((1, d), lambda b, i: (0, 0))],
        out_specs=[pl.BlockSpec((tm, d), lambda b, i: (b * tiles + i, 0)),
                   pl.BlockSpec((None, d4, tm // d4, d), lambda b, i: (b, 0, i, 0)),
                   pl.BlockSpec((None, d16, tm // d16, d), lambda b, i: (b, 0, i, 0))],
        out_shape=[jax.ShapeDtypeStruct((t, d), BF16),
                   jax.ShapeDtypeStruct((bsz, d4, seq // d4, d), BF16),
                   jax.ShapeDtypeStruct((bsz, d16, seq // d16, d), BF16)],
        scratch_shapes=[pltpu.VMEM((d // LANES, tm, LANES), F32)],
        compiler_params=_params("parallel", "parallel"),
        name="rmsnorm_dilated",
    )(x, gain.reshape(1, d))
    return o1, o4.reshape(t, d), o16.reshape(t, d)


def _mm_kernel(a_ref, w_ref, o_ref):
    o_ref[...] = jnp.dot(a_ref[...], w_ref[...], preferred_element_type=F32).astype(o_ref.dtype)


def matmul(a, w, out_dtype, col0=0, n=None, tm=1024, tn=1024):
    m, k = a.shape
    n = w.shape[1] if n is None else n
    j0 = col0 // tn
    assert j0 * tn == col0
    return pl.pallas_call(
        _mm_kernel,
        grid=(m // tm, n // tn),
        in_specs=[pl.BlockSpec((tm, k), lambda i, j: (i, 0)),
                  pl.BlockSpec((k, tn), lambda i, j: (0, j0 + j))],
        out_specs=pl.BlockSpec((tm, tn), lambda i, j: (i, j)),
        out_shape=jax.ShapeDtypeStruct((m, n), out_dtype),
        compiler_params=_params("parallel", "parallel"),
        name="matmul",
    )(a, w)


def _mm_res_norm_kernel(a_ref, w_ref, r_ref, g_ref, o_ref, h_ref, *, n_k):
    k = pl.program_id(1)

    @pl.when(k == 0)
    def _():
        o_ref[...] = r_ref[...]

    o_ref[...] += jnp.dot(a_ref[...], w_ref[...], preferred_element_type=F32)

    @pl.when(k == n_k - 1)
    def _():
        x = o_ref[...]
        ms = jnp.mean(x * x, axis=-1, keepdims=True)
        h_ref[...] = (x * lax.rsqrt(ms + RMS_EPS) * g_ref[...]).astype(h_ref.dtype)


def matmul_residual_norm(a, w, res, gain, tm=256, tk=None):
    m, k = a.shape
    n = w.shape[1]
    tk = k if tk is None else tk
    n_k = k // tk
    return pl.pallas_call(
        functools.partial(_mm_res_norm_kernel, n_k=n_k),
        grid=(m // tm, n_k),
        in_specs=[pl.BlockSpec((tm, tk), lambda i, kk: (i, kk)),
                  pl.BlockSpec((tk, n), lambda i, kk: (kk, 0)),
                  pl.BlockSpec((tm, n), lambda i, kk: (i, 0)),
                  pl.BlockSpec((1, n), lambda i, kk: (0, 0))],
        out_specs=[pl.BlockSpec((tm, n), lambda i, kk: (i, 0)),
                   pl.BlockSpec((tm, n), lambda i, kk: (i, 0))],
        out_shape=[jax.ShapeDtypeStruct((m, n), F32), jax.ShapeDtypeStruct((m, n), BF16)],
        compiler_params=_params("parallel", "arbitrary"),
        name="matmul_residual_norm",
    )(a, w, res, gain.reshape(1, n))


def _dt_kernel(a_ref, w_ref, b_ref, o_ref):
    raw = jnp.dot(a_ref[...], w_ref[...], preferred_element_type=F32) + b_ref[...]
    o_ref[...] = jnp.maximum(raw, 0.0) + jnp.log(1.0 + jnp.exp(-jnp.abs(raw)))


def dt_projection(a, w, bias, tm=1024):
    m, k = a.shape
    n = w.shape[1]
    return pl.pallas_call(
        _dt_kernel,
        grid=(m // tm,),
        in_specs=[pl.BlockSpec((tm, k), lambda i: (i, 0)),
                  pl.BlockSpec((k, n), lambda i: (0, 0)),
                  pl.BlockSpec((1, n), lambda i: (0, 0))],
        out_specs=pl.BlockSpec((tm, n), lambda i: (i, 0)),
        out_shape=jax.ShapeDtypeStruct((m, n), F32),
        compiler_params=_params("parallel"),
        name="dt_projection",
    )(a, w, bias)


def _conv_proj_kernel(a_ref, w_ref, cw_ref, cb_ref, o_ref, acc0, acc1, tail_scr, *, tiles_per_seq, rows=64):
    i = pl.program_id(1)
    tm = a_ref.shape[0]
    halo = tail_scr.shape[0]

    @pl.when(lax.rem(i, tiles_per_seq) == 0)
    def _():
        tail_scr[...] = jnp.zeros(tail_scr.shape, F32)

    def project(dst, cs):
        dst[...] = jnp.dot(a_ref[...], w_ref[:, cs], preferred_element_type=F32)

    def conv_silu(src, cs):
        cw = cw_ref[:, cs]
        bias = cb_ref[:, cs]
        zero = pl.multiple_of(jnp.minimum(i, 0) * 8, 8)
        for c in range(tm // rows):
            r0 = c * rows
            head = tail_scr[:, cs] if c == 0 else src[pl.ds(r0 - halo + zero, halo), :]
            u = src[pl.ds(r0 + zero, rows), :]
            ext = jnp.concatenate([head, u], axis=0)
            y = bias + cw[SSM_CONV_WIDTH - 1:SSM_CONV_WIDTH, :] * u
            for k in range(SSM_CONV_WIDTH - 1):
                shift = SSM_CONV_WIDTH - 1 - k
                y = y + cw[k:k + 1, :] * pltpu.roll(ext, shift, axis=0)[halo:, :]
            o_ref[r0:r0 + rows, cs] = (y * _sigmoid(y)).astype(o_ref.dtype)
        tail_scr[:, cs] = src[tm - halo:, :]

    accs = (acc0, acc1)
    width = acc0.shape[1]
    n_chunks = o_ref.shape[1] // width
    for nc in range(n_chunks):
        cs = slice(nc * width, (nc + 1) * width)
        project(accs[nc % 2], cs)
        if nc > 0:
            conv_silu(accs[(nc - 1) % 2], slice((nc - 1) * width, nc * width))
    conv_silu(accs[(n_chunks - 1) % 2], slice((n_chunks - 1) * width, n_chunks * width))


def conv_projection(a, w, col0, n, conv_w, conv_b, seq, tm=1024, tn=2048, chunk=256):
    m, k = a.shape
    j0 = col0 // tn
    assert j0 * tn == col0
    return pl.pallas_call(
        functools.partial(_conv_proj_kernel, tiles_per_seq=seq // tm),
        grid=(n // tn, m // tm),
        in_specs=[pl.BlockSpec((tm, k), lambda j, i: (i, 0)),
                  pl.BlockSpec((k, tn), lambda j, i: (0, j0 + j)),
                  pl.BlockSpec((SSM_CONV_WIDTH, tn), lambda j, i: (0, j)),
                  pl.BlockSpec((1, tn), lambda j, i: (0, j))],
        out_specs=pl.BlockSpec((tm, tn), lambda j, i: (i, j)),
        out_shape=jax.ShapeDtypeStruct((m, n), BF16),
        scratch_shapes=[pltpu.VMEM((tm, chunk), F32), pltpu.VMEM((tm, chunk), F32),
                        pltpu.VMEM((8, tn), F32)],
        compiler_params=_params("parallel", "arbitrary"),
        name="conv_projection",
    )(a, w, conv_w, conv_b.reshape(1, n))


def _ssd_kernel(xbc_ref, z_ref, dt_ref, a_ref, dskip_ref, nw_ref, expand_ref, o_ref, state_ref,
                *, d_inner):
    c = pl.program_id(1)
    n_groups = SSM_N_GROUPS
    gw = d_inner // n_groups
    pair = 2 * SSM_HEAD_DIM
    lq = SSM_CHUNK

    @pl.when(c == 0)
    def _():
        state_ref[...] = jnp.zeros(state_ref.shape, F32)

    dt = dt_ref[...]
    a = dt * a_ref[...]
    row_i = lax.broadcasted_iota(jnp.int32, (lq, lq), 0)
    col_i = lax.broadcasted_iota(jnp.int32, (lq, lq), 1)
    causal = row_i >= col_i
    tril = jnp.where(causal, 1.0, 0.0).astype(BF16)
    tril3 = jnp.concatenate([tril, tril, tril], axis=1)
    acum = jnp.dot(tril3, jnp.concatenate(_split3(a), axis=0),
                   preferred_element_type=F32)
    acum_t = acum.T
    dt_t = dt.T
    alast = acum[lq - 1:lq, :]
    stacked = jnp.concatenate([
        jnp.exp(acum),
        dt * jnp.exp(alast - acum),
        jnp.broadcast_to(jnp.exp(alast), (8, LANES)),
    ], axis=0)
    wide = jnp.dot(jnp.concatenate(_split3(stacked), axis=1), expand_ref[...],
                   preferred_element_type=F32)
    decay_in_w = wide[0:lq, :]
    dt_decay_out_w = wide[lq:2 * lq, :]
    chunk_decay_w = wide[2 * lq:2 * lq + 1, :]
    lane = lax.broadcasted_iota(jnp.int32, (lq, pair), 1)
    lo_half = lane < SSM_HEAD_DIM

    for g in range(n_groups):
        sl = slice(g * gw, (g + 1) * gw)
        xs = xbc_ref[:, sl]
        bm = xbc_ref[:, d_inner + g * SSM_D_STATE:d_inner + (g + 1) * SSM_D_STATE]
        cm = xbc_ref[:, d_inner + (n_groups + g) * SSM_D_STATE:
                     d_inner + (n_groups + g + 1) * SSM_D_STATE]
        xs32 = xs.astype(F32)
        cb = lax.dot_general(cm, bm, (((1,), (1,)), ((), ())), preferred_element_type=F32)
        state = state_ref[g]
        y = jnp.dot(cm, state.astype(BF16), preferred_element_type=F32) * decay_in_w[:, sl]
        xw = xs * dt_decay_out_w[:, sl].astype(BF16)
        bm_t = bm.astype(F32).T.astype(BF16)
        state_ref[g] = state * chunk_decay_w[:, sl] + jnp.dot(bm_t, xw, preferred_element_type=F32)

        y_pairs = []
        for hp in range(gw // pair):
            xs_pair = xs[:, hp * pair:(hp + 1) * pair]
            zero = jnp.zeros_like(xs_pair)
            mats = []
            for half in range(2):
                h = (g * gw + hp * pair) // SSM_HEAD_DIM + half
                seg = acum[:, h:h + 1] - acum_t[h:h + 1, :]
                decay = jnp.exp(jnp.where(causal, seg, -jnp.inf))
                mats.append((decay * cb * dt_t[h:h + 1, :]).astype(BF16))
            rhs = jnp.concatenate([jnp.where(lo_half, xs_pair, zero),
                                   jnp.where(lo_half, zero, xs_pair)], axis=0)
            y_pairs.append(jnp.dot(jnp.concatenate(mats, axis=1), rhs, preferred_element_type=F32))
        y = y + jnp.concatenate(y_pairs, axis=1)
        y = y + xs32 * dskip_ref[:, sl]
        zg = z_ref[:, sl].astype(F32)
        y = y * (zg * _sigmoid(zg))
        ms = jnp.mean(y * y, axis=-1, keepdims=True)
        o_ref[:, sl] = (y * lax.rsqrt(ms + RMS_EPS) * nw_ref[:, sl]).astype(o_ref.dtype)


def ssd_mixer_core(xbc, z, dt, a_row, dskip_w, norm_w, expand, bsz, seq):
    t, d_inner = z.shape
    n_chunks = seq // SSM_CHUNK
    gw = d_inner // SSM_N_GROUPS
    row = lambda b, c: (b * n_chunks + c, 0)
    fixed = lambda b, c: (0, 0)
    return pl.pallas_call(
        functools.partial(_ssd_kernel, d_inner=d_inner),
        grid=(bsz, n_chunks),
        in_specs=[pl.BlockSpec((SSM_CHUNK, xbc.shape[1]), row),
                  pl.BlockSpec((SSM_CHUNK, d_inner), row),
                  pl.BlockSpec((SSM_CHUNK, LANES), row),
                  pl.BlockSpec((1, LANES), fixed),
                  pl.BlockSpec((1, d_inner), fixed),
                  pl.BlockSpec((1, d_inner), fixed),
                  pl.BlockSpec((3 * LANES, d_inner), fixed)],
        out_specs=pl.BlockSpec((SSM_CHUNK, d_inner), row),
        out_shape=jax.ShapeDtypeStruct((t, d_inner), BF16),
        scratch_shapes=[pltpu.VMEM((SSM_N_GROUPS, SSM_D_STATE, gw), F32)],
        compiler_params=_params("parallel", "arbitrary"),
        name="ssd_mixer_core",
    )(xbc, z, dt, a_row, dskip_w, norm_w, expand)


def _ffn_kernel(h_ref, x_ref, wg_ref, wu_ref, wd_ref, o_ref):
    @pl.when(pl.program_id(1) == 0)
    def _():
        o_ref[...] = x_ref[...]

    h = h_ref[...]
    gate = jnp.dot(h, wg_ref[...], preferred_element_type=F32)
    up = jnp.dot(h, wu_ref[...], preferred_element_type=F32)
    act = (gate * _sigmoid(gate) * up).astype(BF16)
    o_ref[...] += jnp.dot(act, wd_ref[...], preferred_element_type=F32)


def swiglu_ffn_residual(h, x, w_gate, w_up, w_down, layer, tm=512, tf=512):
    m, d = h.shape
    d_ff = w_gate.shape[2]
    return pl.pallas_call(
        _ffn_kernel,
        grid=(m // tm, d_ff // tf),
        in_specs=[pl.BlockSpec((tm, d), lambda i, f: (i, 0)),
                  pl.BlockSpec((tm, d), lambda i, f: (i, 0)),
                  pl.BlockSpec((None, d, tf), lambda i, f: (layer, 0, f)),
                  pl.BlockSpec((None, d, tf), lambda i, f: (layer, 0, f)),
                  pl.BlockSpec((None, tf, d), lambda i, f: (layer, f, 0))],
        out_specs=pl.BlockSpec((tm, d), lambda i, f: (i, 0)),
        out_shape=jax.ShapeDtypeStruct((m, d), F32),
        compiler_params=_params("parallel", "arbitrary"),
        name="swiglu_ffn",
    )(h, x, w_gate, w_up, w_down)


def _qk_kernel(a_ref, w_ref, gain_ref, cos_ref, sin_ref, o_ref, *, rows):
    w = w_ref[...]
    gain = gain_ref[...]
    hd = ATTN_HEAD_DIM
    for c in range(a_ref.shape[0] // rows):
        rs = slice(c * rows, (c + 1) * rows)
        acc = jnp.dot(a_ref[rs, :], w, preferred_element_type=F32)
        cos = cos_ref[rs, :]
        sin = sin_ref[rs, :]
        for h in range(acc.shape[1] // hd):
            t = acc[:, h * hd:(h + 1) * hd]
            ms = jnp.mean(t * t, axis=-1, keepdims=True)
            tn = t * lax.rsqrt(ms + RMS_EPS) * gain
            rot = tn * cos + pltpu.roll(tn, hd // 2, axis=1) * sin
            o_ref[h, rs, :] = rot.astype(o_ref.dtype)


def qk_projection(a, w, col0, n, gains, cos, sin, seq, tm=1024, rows=256):
    m, k = a.shape
    tiles = seq // tm
    j0 = col0 // n
    assert j0 * n == col0
    return pl.pallas_call(
        functools.partial(_qk_kernel, rows=rows),
        grid=(m // tm, 2),
        in_specs=[pl.BlockSpec((tm, k), lambda i, j: (i, 0)),
                  pl.BlockSpec((k, n), lambda i, j: (0, j0 + j)),
                  pl.BlockSpec((None, 1, ATTN_HEAD_DIM), lambda i, j: (j, 0, 0)),
                  pl.BlockSpec((tm, ATTN_HEAD_DIM), lambda i, j: (i % tiles, 0)),
                  pl.BlockSpec((tm, ATTN_HEAD_DIM), lambda i, j: (i % tiles, 0))],
        out_specs=pl.BlockSpec((None, n // ATTN_HEAD_DIM, tm, ATTN_HEAD_DIM), lambda i, j: (j, 0, i, 0)),
        out_shape=jax.ShapeDtypeStruct((2, n // ATTN_HEAD_DIM, m, ATTN_HEAD_DIM), BF16),
        compiler_params=_params("parallel", "arbitrary"),
        name="qk_projection",
    )(a, w, gains, cos, sin)


def _v_kernel(a_ref, w_ref, o_ref):
    acc = jnp.dot(a_ref[...], w_ref[...], preferred_element_type=F32)
    hd = o_ref.shape[2]
    for h in range(o_ref.shape[0]):
        o_ref[h] = acc[:, h * hd:(h + 1) * hd].astype(o_ref.dtype)


def v_projection(a, w, col0, n, tm=1024):
    m, k = a.shape
    j0 = col0 // n
    assert j0 * n == col0
    heads = n // ATTN_HEAD_DIM
    return pl.pallas_call(
        _v_kernel,
        grid=(m // tm,),
        in_specs=[pl.BlockSpec((tm, k), lambda i: (i, 0)),
                  pl.BlockSpec((k, n), lambda i: (0, j0))],
        out_specs=pl.BlockSpec((heads, tm, ATTN_HEAD_DIM), lambda i: (0, i, 0)),
        out_shape=jax.ShapeDtypeStruct((heads, m, ATTN_HEAD_DIM), BF16),
        compiler_params=_params("parallel"),
        name="v_projection",
    )(a, w)


def _attn_kernel(*refs, seq):
    n_groups = len(ATTN_PATTERNS)
    qkv = refs[:3 * n_groups]
    o_ref = refs[3 * n_groups]
    out_scr, lse_scr, vext_all = refs[3 * n_groups + 1:]
    blk = ATTN_BLOCK
    hd = ATTN_HEAD_DIM
    scale = hd ** -0.5
    qi = lax.broadcasted_iota(jnp.int32, (blk, blk), 0)
    kj = lax.broadcasted_iota(jnp.int32, (blk, blk), 1)
    own_ok = kj <= qi
    prev_ok = kj >= qi
    contract_last = (((1,), (1,)), ((), ()))
    for g in range(n_groups):
        vext_all[g, :, hd:] = jnp.ones((seq, hd), BF16)

    def attend(q_ref, k_ref, vext_scr, blocks):
        qs = [q_ref[p0:p0 + blk, :] for p0, _ in blocks]
        s_own = [lax.dot_general(q, k_ref[p0:p0 + blk, :], contract_last, preferred_element_type=F32)
                 for q, (p0, _) in zip(qs, blocks)]
        s_prev = [lax.dot_general(q, k_ref[p0 - blk:p0, :], contract_last, preferred_element_type=F32)
                  if has_prev else None for q, (p0, has_prev) in zip(qs, blocks)]
        results = []
        for so, sp, (p0, has_prev) in zip(s_own, s_prev, blocks):
            so = jnp.where(own_ok, so * scale, -jnp.inf)
            if has_prev:
                sp = jnp.where(prev_ok, sp * scale, -jnp.inf)
                m = jnp.max(jnp.maximum(so, sp), axis=-1, keepdims=True)
            else:
                m = jnp.max(so, axis=-1, keepdims=True)
            ext = jnp.dot(jnp.exp(so - m).astype(BF16), vext_scr[p0:p0 + blk, :],
                          preferred_element_type=F32)
            if has_prev:
                ext = ext + jnp.dot(jnp.exp(sp - m).astype(BF16), vext_scr[p0 - blk:p0, :],
                                    preferred_element_type=F32)
            denom = ext[:, hd:]
            results.append((ext[:, :hd] / denom, m + jnp.log(denom)))
        return results

    batch = seq // blk
    for g, (window, dil) in enumerate(ATTN_PATTERNS):
        assert window // dil == blk
        q_ref, k_ref, v_ref = qkv[3 * g:3 * g + 3]
        vext_scr = vext_all.at[g]
        vext_scr[:, :hd] = v_ref[...]
        nb = seq // dil // blk
        blocks = [((r * nb + i) * blk, i > 0, i * blk * dil + r) for r in range(dil) for i in range(nb)]
        for b0 in range(0, len(blocks), batch):
            chunk = blocks[b0:b0 + batch]
            for (out, lse), (p0, _, t0) in zip(attend(q_ref, k_ref, vext_scr, [c[:2] for c in chunk]), chunk):
                dst = pl.ds(p0, blk) if dil == 1 else pl.ds(t0, blk, stride=dil)
                out_scr[g, dst, :] = out
                lse_scr[g, dst, :] = lse

    lse_all = [lse_scr[g] for g in range(n_groups)]
    top = functools.reduce(jnp.maximum, lse_all)
    num = jnp.zeros((seq, hd), F32)
    den = jnp.zeros((seq, hd), F32)
    for g in range(n_groups):
        wgt = jnp.exp(lse_all[g] - top)
        num = num + wgt * out_scr[g]
        den = den + wgt
    o_ref[...] = (num / den).astype(o_ref.dtype)


def dilated_attention(qk_groups, v_groups, bsz, seq):
    heads, t, _ = v_groups[0].shape
    n = heads * ATTN_HEAD_DIM
    in_specs, args = [], []
    for qk, v in zip(qk_groups, v_groups):
        for which in range(2):
            in_specs.append(pl.BlockSpec((None, None, seq, ATTN_HEAD_DIM),
                                         lambda b, h, which=which: (which, h, b, 0)))
            args.append(qk)
        in_specs.append(pl.BlockSpec((None, seq, ATTN_HEAD_DIM), lambda b, h: (h, b, 0)))
        args.append(v)
    n_groups = len(v_groups)
    return pl.pallas_call(
        functools.partial(_attn_kernel, seq=seq),
        grid=(bsz, heads),
        in_specs=in_specs,
        out_specs=pl.BlockSpec((seq, ATTN_HEAD_DIM), lambda b, h: (b, h)),
        out_shape=jax.ShapeDtypeStruct((t, n), BF16),
        scratch_shapes=[pltpu.VMEM((n_groups, seq, ATTN_HEAD_DIM), F32),
                        pltpu.VMEM((n_groups, seq, ATTN_HEAD_DIM), F32),
                        pltpu.VMEM((n_groups, seq, 2 * ATTN_HEAD_DIM), BF16)],
        compiler_params=_params("parallel", "parallel"),
        name="dilated_attention",
    )(*args)


def _dilate_rows(table, dil):
    seq, c = table.shape
    return table.reshape(seq // dil, dil, c).transpose(1, 0, 2).reshape(seq, c)


def _rope_tables(seq):
    half = ATTN_HEAD_DIM // 2
    inv_freq = ROPE_THETA ** (-jnp.arange(half, dtype=F32) / half)
    ang = jnp.arange(seq, dtype=F32)[:, None] * inv_freq[None, :]
    cos, sin = jnp.cos(ang), jnp.sin(ang)
    return jnp.concatenate([cos, cos], axis=1), jnp.concatenate([-sin, sin], axis=1)


def mamba2_layer(x, h, bsz, seq, w_in, conv_w, conv_b, dt_bias, a_log, d_skip, norm_w, w_out, next_gain):
    d_inner = w_out.shape[0]
    n_heads = d_inner // SSM_HEAD_DIM
    conv_dim = conv_w.shape[1]
    d_model = w_in.shape[0]
    w_in_bf = w_in.astype(BF16)
    w_dt = jnp.zeros((d_model, LANES), BF16).at[:, :n_heads].set(w_in_bf[:, d_inner + conv_dim:])
    dt_bias_row = jnp.zeros((1, LANES), F32).at[0, :n_heads].set(dt_bias.astype(F32))
    a_row = jnp.zeros((1, LANES), F32).at[0, :n_heads].set(-jnp.exp(a_log.astype(F32)))
    dskip_w = jnp.repeat(d_skip.astype(F32), SSM_HEAD_DIM).reshape(1, d_inner)
    head_of_channel = jnp.arange(d_inner) // SSM_HEAD_DIM
    expand = (jnp.arange(LANES)[:, None] == head_of_channel[None, :]).astype(BF16)
    expand = jnp.concatenate([expand, expand, expand], axis=0)

    z = matmul(h, w_in_bf, BF16, col0=0, n=d_inner)
    xbc = conv_projection(h, w_in_bf, d_inner, conv_dim, conv_w.astype(F32), conv_b.astype(F32), seq)
    dt = dt_projection(h, w_dt, dt_bias_row)
    y = ssd_mixer_core(xbc, z, dt, a_row, dskip_w, norm_w.astype(F32).reshape(1, d_inner), expand,
                       bsz, seq)
    return matmul_residual_norm(y, w_out.astype(BF16), x, next_gain)


def attention_layer(x, hs, bsz, seq, w_qkv, q_norm, k_norm, w_out, next_gain):
    d_model = w_qkv.shape[0]
    n_groups = len(ATTN_PATTERNS)
    n = ATTN_GROUP_HEADS * ATTN_HEAD_DIM
    w = w_qkv.astype(BF16)
    cos, sin = _rope_tables(seq)
    qk_groups, v_groups = [], []
    for g, (_, dil) in enumerate(ATTN_PATTERNS):
        gains = jnp.stack([q_norm[g], k_norm[g]]).astype(F32).reshape(2, 1, ATTN_HEAD_DIM)
        col0 = g * 3 * n
        qk_groups.append(qk_projection(hs[g], w, col0, n, gains, _dilate_rows(cos, dil),
                                       _dilate_rows(sin, dil), seq))
        v_groups.append(v_projection(hs[g], w, col0 + 2 * n, n))
    merged = dilated_attention(qk_groups, v_groups, bsz, seq)
    return matmul_residual_norm(merged, w_out.astype(BF16), x, next_gain)


def kernel(x, mix_norm, ffn_norm, ssm_w_in, ssm_conv_w, ssm_conv_b, ssm_dt_bias, ssm_a_log, ssm_d, ssm_norm, ssm_w_out, attn_w_qkv, attn_q_norm, attn_k_norm, attn_w_out, ffn_w_gate, ffn_w_up, ffn_w_down):
    bsz, seq, d_model = x.shape
    depth = mix_norm.shape[0]
    xt = x.reshape(bsz * seq, d_model).astype(F32)
    w_gate, w_up, w_down = ffn_w_gate.astype(BF16), ffn_w_up.astype(BF16), ffn_w_down.astype(BF16)
    for layer in range(depth):
        j = layer // 2
        if layer % 2 == 0:
            h = rmsnorm(xt, mix_norm[layer].astype(F32))
            xt, h = mamba2_layer(xt, h, bsz, seq, ssm_w_in[j], ssm_conv_w[j], ssm_conv_b[j], ssm_dt_bias[j],
                                 ssm_a_log[j], ssm_d[j], ssm_norm[j], ssm_w_out[j], ffn_norm[layer].astype(F32))
        else:
            hs = rmsnorm_dilated(xt, mix_norm[layer].astype(F32), bsz, seq)
            xt, h = attention_layer(xt, hs, bsz, seq, attn_w_qkv[j], attn_q_norm[j], attn_k_norm[j],
                                    attn_w_out[j], ffn_norm[layer].astype(F32))
        xt = swiglu_ffn_residual(h, xt, w_gate, w_up, w_down, layer)
    return xt.reshape(bsz, seq, d_model).astype(x.dtype)
```
